```python
import jax, jax.numpy as jnp
from jax import lax
import numpy as np

D_MODEL = 4096
BATCH = 4
SEQ = 2048
DEPTH = 4

N_A_LAYERS = DEPTH // 2
N_B_LAYERS = DEPTH - N_A_LAYERS

RWKV_HEAD = 64
RWKV_HEADS = D_MODEL // RWKV_HEAD
DECAY_LORA = max(32, int(round(1.8 * D_MODEL ** 0.5 / 32)) * 32)
ICLR_LORA = max(32, int(round(1.8 * D_MODEL ** 0.5 / 32)) * 32)
MV_LORA = max(32, int(round(1.3 * D_MODEL ** 0.5 / 32)) * 32)
GATE_LORA = max(32, int(round(0.6 * D_MODEL ** 0.8 / 32)) * 32)
GN_EPS = 64e-5

HEAD_DIM = 128
GROUP_HEADS = D_MODEL // 256
DILATION_PAIRS = ((128, 1), (512, 4), (2048, 16))
N_GROUPS = len(DILATION_PAIRS)
Q_WIDTH = N_GROUPS * GROUP_HEADS * HEAD_DIM
OUT_WIDTH = GROUP_HEADS * HEAD_DIM
ATTN_BLOCK = 128

D_FF = 256 * ((8 * D_MODEL // 3 + 255) // 256)
CONV_WIDTH = 3
NORM_EPS = 1e-6

kernel_name = "yoco_rwkv7_dilated_alibi_convffn"


def rms_norm(x, gain, out_dtype=None):
    xf = x.astype(jnp.float32)
    y = xf * lax.rsqrt(jnp.mean(xf * xf, axis=-1, keepdims=True) + NORM_EPS) * gain.astype(jnp.float32)
    return y.astype(x.dtype if out_dtype is None else out_dtype)


def token_shift(x, lag=1):
    return jnp.pad(x, ((0, 0), (lag, 0), (0, 0)))[:, : x.shape[1]]


def alibi_slopes(n):
    return jnp.exp2(-8.0 * jnp.arange(1, n + 1, dtype=jnp.float32) / n)


def wkv7_scan(r, w, k, v, a, b):
    B, T, H, N = r.shape

    def step(S, inp):
        r_t, w_t, k_t, v_t, a_t, b_t = inp
        sa = jnp.einsum('bhvk,bhk->bhv', S, a_t)
        S = S * w_t[:, :, None, :] + sa[..., None] * b_t[:, :, None, :] + v_t[..., None] * k_t[:, :, None, :]
        return S, jnp.einsum('bhvk,bhk->bhv', S, r_t)

    S0 = jnp.zeros((B, H, N, N), jnp.float32)
    xs = tuple(t.transpose(1, 0, 2, 3) for t in (r, w, k, v, a, b))
    _, ys = lax.scan(step, S0, xs)
    return ys.transpose(1, 0, 2, 3)


def rwkv7_time_mix(h, mu, w_r, w_k, w_v, w_o, w0, w1, w2, a0, a1, a2, g1, g2,
                   k_k, k_a, r_k, ln_w, ln_b, v_first, v_mix):
    B, T, C = h.shape
    H, N = RWKV_HEADS, RWKV_HEAD
    xx = token_shift(h) - h
    xr, xw, xk, xv, xa, xg = (h + xx * mu[i] for i in range(6))
    r = xr @ w_r
    k = xk @ w_k
    v = xv @ w_v
    w = -jax.nn.softplus(-(w0 + jnp.tanh(xw @ w1) @ w2)) - 0.5
    if v_mix is None:
        v_first = v
    else:
        v0, v1, v2 = v_mix
        v = v + (v_first - v) * jax.nn.sigmoid(v0 + (xv @ v1) @ v2)
    a = jax.nn.sigmoid(a0 + (xa @ a1) @ a2)
    g = jax.nn.sigmoid(xg @ g1) @ g2
    heads = lambda t: t.reshape(B, T, H, N).astype(jnp.float32)
    kk = heads(k * k_k)
    kk = kk / jnp.maximum(jnp.sqrt(jnp.sum(kk * kk, axis=-1, keepdims=True)), 1e-12)
    k = k * (1 + (a - 1) * k_a)
    rh, kh, vh, ah = heads(r), heads(k), heads(v), heads(a)
    decay = jnp.exp(-jnp.exp(heads(w)))
    y = wkv7_scan(rh, decay, kh, vh, -kk, kk * ah)
    mean = jnp.mean(y, axis=-1, keepdims=True)
    var = jnp.mean(jnp.square(y - mean), axis=-1, keepdims=True)
    yn = ((y - mean) * lax.rsqrt(var + GN_EPS)).reshape(B, T, C) * ln_w.astype(jnp.float32) + ln_b.astype(jnp.float32)
    bonus = jnp.sum(rh * kh * r_k.astype(jnp.float32), axis=-1, keepdims=True) * vh
    out = (yn + bonus.reshape(B, T, C)).astype(h.dtype)
    return (out * g) @ w_o, v_first


def dilated_branch(q, k, v, slopes, window, dilation):
    B, S, H, Dh = q.shape
    L = S // dilation
    span = window // dilation
    nb = -(-L // ATTN_BLOCK)
    Lp = nb * ATTN_BLOCK
    by_res = lambda t: t.reshape(B, L, dilation, H, Dh).transpose(0, 2, 1, 3, 4)
    qr, kr, vr = by_res(q), by_res(k), by_res(v)
    qb = jnp.pad(qr, ((0, 0), (0, 0), (0, Lp - L), (0, 0), (0, 0))).reshape(B, dilation, nb, ATTN_BLOCK, H, Dh)

    def key_blocks(t):
        tp = jnp.pad(t, ((0, 0), (0, 0), (ATTN_BLOCK, Lp - L), (0, 0), (0, 0)))
        prev = tp[:, :, :Lp].reshape(B, dilation, nb, ATTN_BLOCK, H, Dh)
        cur = tp[:, :, ATTN_BLOCK:].reshape(B, dilation, nb, ATTN_BLOCK, H, Dh)
        return jnp.concatenate([prev, cur], axis=3)

    kb, vb = key_blocks(kr), key_blocks(vr)
    s = jnp.einsum('brnqhd,brnkhd->brnhqk', qb, kb)
    qi = jnp.arange(ATTN_BLOCK)[:, None]
    kj = jnp.arange(2 * ATTN_BLOCK)[None, :] - ATTN_BLOCK
    dist = qi - kj
    blk_start = jnp.arange(nb)[:, None, None] * ATTN_BLOCK
    valid = (dist >= 0) & (dist <= span) & (blk_start + kj >= 0)
    bias = -slopes[:, None, None] * (dist * dilation).astype(jnp.float32)
    s = jnp.where(valid[None, None, :, None], s + bias[None, None, None], -jnp.inf)
    lse = jax.nn.logsumexp(s, axis=-1)
    p = jnp.exp(s - lse[..., None])
    o = jnp.einsum('brnhqk,brnkhd->brnqhd', p, vb)
    o = o.reshape(B, dilation, Lp, H, Dh)[:, :, :L].transpose(0, 2, 1, 3, 4).reshape(B, S, H, Dh)
    lse = lse.transpose(0, 1, 2, 4, 3).reshape(B, dilation, Lp, H)[:, :, :L].transpose(0, 2, 1, 3).reshape(B, S, H)
    return o, lse


def shared_kv(x, kv_norm, w_kv, k_gain):
    B, S, _ = x.shape
    kv = (rms_norm(x, kv_norm) @ w_kv).reshape(B, S, 2, N_GROUPS, GROUP_HEADS, HEAD_DIM)
    k = rms_norm(kv[:, :, 0], k_gain[:, None, :], out_dtype=jnp.float32)
    v = kv[:, :, 1].astype(jnp.float32)
    return k, v


def dilated_attention(h, k, v, w_q, q_gain, w_o):
    B, S, _ = h.shape
    q = (h @ w_q).reshape(B, S, N_GROUPS, GROUP_HEADS, HEAD_DIM)
    q = rms_norm(q, q_gain[:, None, :], out_dtype=jnp.float32) * HEAD_DIM ** -0.5
    slopes = alibi_slopes(GROUP_HEADS)
    outs, lses = [], []
    for g, (window, dilation) in enumerate(DILATION_PAIRS):
        o, l = dilated_branch(q[:, :, g], k[:, :, g], v[:, :, g], slopes, window, dilation)
        outs.append(o)
        lses.append(l)
    alpha = jax.nn.softmax(jnp.stack(lses, axis=0), axis=0)
    o = sum(alpha[g][..., None] * outs[g] for g in range(N_GROUPS))
    return o.reshape(B, S, OUT_WIDTH).astype(h.dtype) @ w_o


def conv_ffn(h, w_in, conv_w, conv_b, w_out):
    u = h @ w_in
    u = conv_w[0] * u + conv_w[1] * token_shift(u, 1) + conv_w[2] * token_shift(u, 2) + conv_b
    gate, up = jnp.split(u, 2, axis=-1)
    return (jax.nn.silu(gate) * up) @ w_out


def setup_inputs(seed: int = 0) -> dict:
    key = jax.random.key(seed)
    ks = iter(jax.random.split(key, 48))
    D, F, NA, NB = D_MODEL, D_FF, N_A_LAYERS, N_B_LAYERS

    def normal(shape, std):
        return std * jax.random.normal(next(ks), shape, jnp.float32)

    def gain(shape):
        return 1.0 + normal(shape, 0.02)

    inp = {}
    inp["x"] = normal((BATCH, SEQ, D), 1.0)
    inp["c"] = normal((BATCH, D), 1.0)
    inp["ada_w"] = normal((D, 6 * D), 0.1 * D ** -0.5)
    inp["ada_table"] = normal((DEPTH, 6, D), 0.05) + jnp.array([0, 0, 1, 0, 0, 1], jnp.float32)[None, :, None]
    inp["norm_mix"] = gain((DEPTH, D))
    inp["norm_ffn"] = gain((DEPTH, D))
    inp["rwkv_mu"] = jax.random.uniform(next(ks), (NA, 6, D), jnp.float32)
    inp["rwkv_w_r"] = normal((NA, D, D), D ** -0.5)
    inp["rwkv_w_k"] = normal((NA, D, D), D ** -0.5)
    inp["rwkv_w_v"] = normal((NA, D, D), D ** -0.5)
    inp["rwkv_w_o"] = normal((NA, D, D), D ** -0.5)
    inp["rwkv_w0"] = jax.random.uniform(next(ks), (NA, D), jnp.float32, -6.0, 1.0)
    inp["rwkv_w1"] = normal((NA, D, DECAY_LORA), D ** -0.5)
    inp["rwkv_w2"] = normal((NA, DECAY_LORA, D), 0.1 * DECAY_LORA ** -0.5)
    inp["rwkv_a0"] = normal((NA, D), 0.1)
    inp["rwkv_a1"] = normal((NA, D, ICLR_LORA), D ** -0.5)
    inp["rwkv_a2"] = normal((NA, ICLR_LORA, D), 0.1 * ICLR_LORA ** -0.5)
    inp["rwkv_g1"] = normal((NA, D, GATE_LORA), D ** -0.5)
    inp["rwkv_g2"] = normal((NA, GATE_LORA, D), GATE_LORA ** -0.5)
    inp["rwkv_k_k"] = 0.85 + normal((NA, D), 0.02)
    inp["rwkv_k_a"] = gain((NA, D))
    inp["rwkv_r_k"] = normal((NA, RWKV_HEADS, RWKV_HEAD), 0.1)
    inp["rwkv_ln_w"] = gain((NA, D))
    inp["rwkv_ln_b"] = normal((NA, D), 0.02)
    inp["rwkv_v0"] = gain((NA - 1, D))
    inp["rwkv_v1"] = normal((NA - 1, D, MV_LORA), D ** -0.5)
    inp["rwkv_v2"] = normal((NA - 1, MV_LORA, D), 0.1 * MV_LORA ** -0.5)
    inp["kv_norm"] = gain((D,))
    inp["attn_w_kv"] = normal((D, 2 * Q_WIDTH), D ** -0.5)
    inp["attn_k_gain"] = gain((N_GROUPS, HEAD_DIM))
    inp["attn_w_q"] = normal((NB, D, Q_WIDTH), D ** -0.5)
    inp["attn_q_gain"] = gain((NB, N_GROUPS, HEAD_DIM))
    inp["attn_w_o"] = normal((NB, OUT_WIDTH, D), OUT_WIDTH ** -0.5)
    inp["ffn_w_in"] = normal((DEPTH, D, 2 * F), D ** -0.5)
    inp["ffn_conv_w"] = normal((DEPTH, CONV_WIDTH, 2 * F), CONV_WIDTH ** -0.5)
    inp["ffn_conv_b"] = normal((DEPTH, 2 * F), 0.02)
    inp["ffn_w_out"] = normal((DEPTH, F, D), F ** -0.5)
    return inp


def reference(x, c, ada_w, ada_table, norm_mix, norm_ffn,
              rwkv_mu, rwkv_w_r, rwkv_w_k, rwkv_w_v, rwkv_w_o, rwkv_w0, rwkv_w1, rwkv_w2,
              rwkv_a0, rwkv_a1, rwkv_a2, rwkv_g1, rwkv_g2, rwkv_k_k, rwkv_k_a, rwkv_r_k,
              rwkv_ln_w, rwkv_ln_b, rwkv_v0, rwkv_v1, rwkv_v2,
              kv_norm, attn_w_kv, attn_k_gain, attn_w_q, attn_q_gain, attn_w_o,
              ffn_w_in, ffn_conv_w, ffn_conv_b, ffn_w_out):
    B, S, D = x.shape
    cond = (jax.nn.silu(c) @ ada_w).reshape(B, 6, D)
    v_first = None
    k_shared = v_shared = None
    for layer in range(DEPTH):
        mod = cond + ada_table[layer]
        sh_mix, sc_mix, gate_mix, sh_ffn, sc_ffn, gate_ffn = (mod[:, i, None, :] for i in range(6))
        h = rms_norm(x, norm_mix[layer]) * (1 + sc_mix) + sh_mix
        if layer < N_A_LAYERS:
            i = layer
            v_mix = None if i == 0 else (rwkv_v0[i - 1], rwkv_v1[i - 1], rwkv_v2[i - 1])
            y, v_first = rwkv7_time_mix(h, rwkv_mu[i], rwkv_w_r[i], rwkv_w_k[i], rwkv_w_v[i], rwkv_w_o[i],
                                        rwkv_w0[i], rwkv_w1[i], rwkv_w2[i], rwkv_a0[i], rwkv_a1[i], rwkv_a2[i],
                                        rwkv_g1[i], rwkv_g2[i], rwkv_k_k[i], rwkv_k_a[i], rwkv_r_k[i],
                                        rwkv_ln_w[i], rwkv_ln_b[i], v_first, v_mix)
        else:
            j = layer - N_A_LAYERS
            if j == 0:
                k_shared, v_shared = shared_kv(x, kv_norm, attn_w_kv, attn_k_gain)
            y = dilated_attention(h, k_shared, v_shared, attn_w_q[j], attn_q_gain[j], attn_w_o[j])
        x = x + gate_mix * y
        h = rms_norm(x, norm_ffn[layer]) * (1 + sc_ffn) + sh_ffn
        x = x + gate_ffn * conv_ffn(h, ffn_w_in[layer], ffn_conv_w[layer], ffn_conv_b[layer], ffn_w_out[layer])
    return x
```

```python
import functools
import math

import jax
import jax.numpy as jnp
from jax import lax
from jax.experimental import pallas as pl
from jax.experimental.pallas import tpu as pltpu

F32 = jnp.float32
BF16 = jnp.bfloat16

VMEM_LIMIT_BYTES = 56 * 1024 * 1024
LANES = 128
SUBLANES = 8

NORM_EPS = 1e-6
GN_EPS = 64e-5
RWKV_HEAD = 64
WKV_CHUNK = 64
HEAD_DIM = 128
ATTN_BLOCK = 128
DILATION_PAIRS = ((128, 1), (512, 4), (2048, 16))
CONV_WIDTH = 3
NEG_BIG = -1e30


def _params(*sem):
    return pltpu.CompilerParams(dimension_semantics=sem, vmem_limit_bytes=VMEM_LIMIT_BYTES)


def _dot(a, b):
    return jnp.dot(a.astype(BF16), b.astype(BF16), preferred_element_type=F32)


def _dot_nt(a, b):
    return lax.dot_general(a.astype(BF16), b.astype(BF16), (((1,), (1,)), ((), ())),
                           preferred_element_type=F32)


def _dot_tn(a, b):
    return lax.dot_general(a.astype(BF16), b.astype(BF16), (((0,), (0,)), ((), ())),
                           preferred_element_type=F32)


def _split_hi_lo(x):
    hi = x.astype(BF16)
    lo = (x - hi.astype(F32)).astype(BF16)
    return hi, lo


def _ada_kernel(c_ref, w_ref, o_ref):
    c = c_ref[...]
    o_ref[...] = _dot(c * jax.nn.sigmoid(c), w_ref[...])


def ada_cond(c, ada_w, tn=512):
    b, d = c.shape
    n = ada_w.shape[1]
    cp = jnp.pad(c, ((0, SUBLANES - b), (0, 0)))
    out = pl.pallas_call(
        _ada_kernel,
        out_shape=jax.ShapeDtypeStruct((SUBLANES, n), F32),
        grid=(n // tn,),
        in_specs=[pl.BlockSpec((SUBLANES, d), lambda j: (0, 0)),
                  pl.BlockSpec((d, tn), lambda j: (0, j))],
        out_specs=pl.BlockSpec((SUBLANES, tn), lambda j: (0, j)),
        compiler_params=_params("parallel"),
        name="ada_cond",
    )(cp, ada_w)
    return out[:b]


def _rms(x):
    return x * lax.rsqrt(jnp.mean(x * x, axis=-1, keepdims=True) + NORM_EPS)


def _norm_affine_kernel(x_ref, a_ref, b_ref, *o_refs):
    xn = _rms(x_ref[0])
    for k, o_ref in enumerate(o_refs):
        o_ref[0] = (xn * a_ref[0, k:k + 1, :] + b_ref[0, k:k + 1, :]).astype(o_ref.dtype)


def norm_affine(x, a, b, tm=256):
    bsz, s, d = x.shape
    n_out = a.shape[1]
    outs = pl.pallas_call(
        _norm_affine_kernel,
        out_shape=[jax.ShapeDtypeStruct((bsz, s, d), BF16)] * n_out,
        grid=(bsz, s // tm),
        in_specs=[pl.BlockSpec((1, tm, d), lambda bi, i: (bi, i, 0)),
                  pl.BlockSpec((1, n_out, d), lambda bi, i: (bi, 0, 0)),
                  pl.BlockSpec((1, n_out, d), lambda bi, i: (bi, 0, 0))],
        out_specs=[pl.BlockSpec((1, tm, d), lambda bi, i: (bi, i, 0))] * n_out,
        compiler_params=_params("parallel", "parallel"),
        name="norm_affine",
    )(x, a, b)
    return outs


def _norm_mix_kernel(x_ref, xp_ref, a_ref, b_ref, mu_ref, *o_refs):
    i = pl.program_id(1)
    a = a_ref[0]
    b = b_ref[0]
    h = _rms(x_ref[0]) * a + b
    hp = _rms(xp_ref[0]) * a + b
    prev_row = jnp.where(i > 0, hp[SUBLANES - 1:SUBLANES, :], 0.0)
    row = lax.broadcasted_iota(jnp.int32, h.shape, 0)
    h_prev = jnp.where(row == 0, prev_row, pltpu.roll(h, 1, axis=0))
    xx = h_prev - h
    for k, o_ref in enumerate(o_refs):
        o_ref[0] = (h + xx * mu_ref[k:k + 1, :]).astype(o_ref.dtype)


def norm_token_mix(x, a, b, mu, tm=256):
    bsz, s, d = x.shape
    n_out = mu.shape[0]
    rows_per_blk = tm // SUBLANES
    outs = pl.pallas_call(
        _norm_mix_kernel,
        out_shape=[jax.ShapeDtypeStruct((bsz, s, d), BF16)] * n_out,
        grid=(bsz, s // tm),
        in_specs=[pl.BlockSpec((1, tm, d), lambda bi, i: (bi, i, 0)),
                  pl.BlockSpec((1, SUBLANES, d),
                               lambda bi, i: (bi, jnp.maximum(i * rows_per_blk - 1, 0), 0)),
                  pl.BlockSpec((1, 1, d), lambda bi, i: (bi, 0, 0)),
                  pl.BlockSpec((1, 1, d), lambda bi, i: (bi, 0, 0)),
                  pl.BlockSpec((n_out, d), lambda bi, i: (0, 0))],
        out_specs=[pl.BlockSpec((1, tm, d), lambda bi, i: (bi, i, 0))] * n_out,
        compiler_params=_params("parallel", "parallel"),
        name="norm_mix",
    )(x, x, a, b, mu)
    return outs


def _mm_kernel(x_ref, w_ref, *rest, epilogue, n_extra, nk):
    extras = rest[:n_extra]
    if nk == 1:
        outs = rest[n_extra:]
        epilogue(_dot(x_ref[...], w_ref[...]), extras, outs)
        return
    outs = rest[n_extra:-1]
    acc_ref = rest[-1]
    kk = pl.program_id(2)
    part = _dot(x_ref[...], w_ref[...])

    @pl.when(kk == 0)
    def _():
        acc_ref[...] = part

    @pl.when(kk > 0)
    def _():
        acc_ref[...] += part

    @pl.when(kk == nk - 1)
    def _():
        epilogue(acc_ref[...], extras, outs)


def matmul(x, w, epilogue, out_dtypes, *, n, col_off=0, tm, tn, nk=1, extras=()):
    m, k = x.shape
    tk = k // nk
    assert m % tm == 0 and n % tn == 0 and col_off % tn == 0 and k % nk == 0
    joff = col_off // tn
    in_specs = [pl.BlockSpec((tm, tk), lambda i, j, kk: (i, kk)),
                pl.BlockSpec((tk, tn), lambda i, j, kk: (kk, j + joff))]
    args = [x, w]
    for arr, kind in extras:
        if kind == "tile":
            in_specs.append(pl.BlockSpec((tm, tn), lambda i, j, kk: (i, j)))
        elif kind == "col":
            in_specs.append(pl.BlockSpec((1, tn), lambda i, j, kk: (0, j)))
        else:
            seq = m // arr.shape[0]
            in_specs.append(pl.BlockSpec((1, 1, tn), lambda i, j, kk, seq=seq: ((i * tm) // seq, 0, j)))
        args.append(arr)
    scratch = [] if nk == 1 else [pltpu.VMEM((tm, tn), F32)]
    outs = pl.pallas_call(
        functools.partial(_mm_kernel, epilogue=epilogue, n_extra=len(extras), nk=nk),
        out_shape=[jax.ShapeDtypeStruct((m, n), dt) for dt in out_dtypes],
        grid=(m // tm, n // tn, nk),
        in_specs=in_specs,
        out_specs=[pl.BlockSpec((tm, tn), lambda i, j, kk: (i, j))] * len(out_dtypes),
        scratch_shapes=scratch,
        compiler_params=_params("parallel", "parallel", "arbitrary"),
        name="matmul_" + epilogue.__name__.strip("_"),
    )(*args)
    return outs


def _ep_plain(acc, extras, outs):
    outs[0][...] = acc.astype(outs[0].dtype)


def _ep_residual_gate(acc, extras, outs):
    resid_ref, gate_ref = extras
    outs[0][...] = resid_ref[...] + gate_ref[0] * acc


def _ep_head_rms(acc, extras, outs, scale):
    gain_ref, = extras
    tn = acc.shape[1]
    for h in range(tn // HEAD_DIM):
        sl = slice(h * HEAD_DIM, (h + 1) * HEAD_DIM)
        blk = acc[:, sl]
        y = blk * lax.rsqrt(jnp.mean(blk * blk, axis=-1, keepdims=True) + NORM_EPS) * gain_ref[:, sl]
        if scale != 1.0:
            y = y * scale
        outs[0][:, sl] = y.astype(outs[0].dtype)


def _ep_q_norm(acc, extras, outs):
    _ep_head_rms(acc, extras, outs, HEAD_DIM ** -0.5)


def _ep_k_norm(acc, extras, outs):
    _ep_head_rms(acc, extras, outs, 1.0)


def _lora_kernel(x_ref, w1_ref, w2_ref, bias_ref, o_ref, *, act, post):
    t = _dot(x_ref[...], w1_ref[...])
    if act == "tanh":
        t = jnp.tanh(t)
    elif act == "sigmoid":
        t = jax.nn.sigmoid(t)
    z = bias_ref[...] + _dot(t, w2_ref[...])
    if post == "sigmoid":
        z = jax.nn.sigmoid(z)
    elif post == "log_decay":
        z = -math.exp(-0.5) * jax.nn.sigmoid(z)
    o_ref[...] = z.astype(o_ref.dtype)


def lora(x, w1, w2, bias, *, act, post, out_dtype=F32, tm=512):
    m, k = x.shape
    r = w1.shape[1]
    n = w2.shape[1]
    tm = min(tm, m)
    return pl.pallas_call(
        functools.partial(_lora_kernel, act=act, post=post),
        out_shape=jax.ShapeDtypeStruct((m, n), out_dtype),
        grid=(m // tm,),
        in_specs=[pl.BlockSpec((tm, k), lambda i: (i, 0)),
                  pl.BlockSpec((k, r), lambda i: (0, 0)),
                  pl.BlockSpec((r, n), lambda i: (0, 0)),
                  pl.BlockSpec((1, n), lambda i: (0, 0))],
        out_specs=pl.BlockSpec((tm, n), lambda i: (i, 0)),
        compiler_params=_params("parallel"),
        name="lora_" + act + "_" + post,
    )(x, w1, w2, bias)


def _wkv_kernel(*refs, n_pairs, has_vmix):
    if has_vmix:
        (r_ref, k_ref, v_ref, lw_ref, a_ref, g_ref, vf_ref, vg_ref,
         kk_ref, ka_ref, rk_ref, lnw_ref, lnb_ref, o_ref, state_ref) = refs
    else:
        (r_ref, k_ref, v_ref, lw_ref, a_ref, g_ref,
         kk_ref, ka_ref, rk_ref, lnw_ref, lnb_ref, o_ref, state_ref) = refs
    C = WKV_CHUNK
    P2 = 2 * C
    c_idx = pl.program_id(2)

    @pl.when(c_idx == 0)
    def _():
        state_ref[...] = jnp.zeros_like(state_ref)

    lane = lax.broadcasted_iota(jnp.int32, (C, LANES), 1)
    lo_half = lane < RWKV_HEAD
    ri = lax.broadcasted_iota(jnp.int32, (P2, P2), 0)
    ci = lax.broadcasted_iota(jnp.int32, (P2, P2), 1)
    same_head = (ri < C) == (ci < C)
    strict = jnp.logical_and(same_head, ci < ri)
    incl = jnp.logical_and(same_head, ci <= ri)
    block_ones = jnp.where(same_head, 1.0, 0.0).astype(BF16)
    tri_r = lax.broadcasted_iota(jnp.int32, (C, C), 0)
    tri_c = lax.broadcasted_iota(jnp.int32, (C, C), 1)
    tril_ones = jnp.where(tri_c <= tri_r, 1.0, 0.0).astype(BF16)
    ones_sq = jnp.ones((P2, P2), BF16)

    def stack(x):
        return jnp.concatenate([jnp.where(lo_half, x, 0.0), jnp.where(lo_half, 0.0, x)], axis=0)

    def head_sum(x):
        hi, lo = _split_hi_lo(x)
        return (jnp.dot(hi, block_ones, preferred_element_type=F32)
                + jnp.dot(lo, block_ones, preferred_element_type=F32))

    for p in range(n_pairs):
        cs = slice(p * LANES, (p + 1) * LANES)
        r = r_ref[0, :, cs]
        k = k_ref[0, :, cs]
        v = v_ref[0, :, cs]
        lw = lw_ref[0, :, cs]
        a = a_ref[0, :, cs]
        if has_vmix:
            v = v + (vf_ref[0, :, cs] - v) * vg_ref[0, :, cs]
        kkr = k * kk_ref[:, cs]
        nrm = jnp.sqrt(head_sum(kkr * kkr))
        kk = kkr / jnp.maximum(nrm, 1e-12)
        k2 = k * (1.0 + (a - 1.0) * ka_ref[:, cs])

        lw_hi, lw_lo = _split_hi_lo(lw)
        cum = (jnp.dot(tril_ones, lw_hi, preferred_element_type=F32)
               + jnp.dot(tril_ones, lw_lo, preferred_element_type=F32))
        cum_end = cum[C - 1:C, :]
        lw_cat = jnp.concatenate([lw_hi, lw_lo], axis=0)
        cum_end_col = lax.dot_general(lw_cat, ones_sq, (((0,), (0,)), ((), ())),
                                      preferred_element_type=F32)

        e_prev = jnp.exp(cum - lw)
        e_cum = jnp.exp(cum)
        e_neg = jnp.exp(-cum)
        e_tail = jnp.exp(cum_end - cum)
        b = kk * a
        a_t = stack(-kk * e_prev)
        r_t = stack(r * e_cum)
        b_t = stack(b * e_neg)
        k_t = stack(k2 * e_neg)
        b_g = stack(b * e_tail)
        k_g = stack(k2 * e_tail)
        v_s = stack(v)

        gram = _dot_nt(jnp.concatenate([a_t, r_t], axis=0), jnp.concatenate([b_t, k_t], axis=0))
        p_ab = jnp.where(strict, gram[:P2, :P2], 0.0)
        p_ak = jnp.where(strict, gram[:P2, P2:], 0.0)
        p_rb = jnp.where(incl, gram[P2:, :P2], 0.0)
        p_rk = jnp.where(incl, gram[P2:, P2:], 0.0)

        x = jnp.concatenate([a_t, _dot(p_ak, v_s)], axis=1)
        pw = p_ab
        n_fac = int(math.log2(C))
        for i in range(n_fac):
            x = x + _dot(pw, x)
            if i + 1 < n_fac:
                pw = _dot(pw, pw)

        rhs = jnp.concatenate([x, jnp.concatenate([jnp.zeros_like(v_s), v_s], axis=1)], axis=0)
        ry = _dot(jnp.concatenate([p_rb, p_rk], axis=1), rhs)
        gh = _dot_tn(jnp.concatenate([b_g, k_g], axis=0), rhs)

        m0 = state_ref[p]
        sm = _dot(jnp.concatenate([r_t + ry[:, :P2], gh[:, :P2]], axis=0), m0)
        y_s = sm[:P2] + ry[:, P2:]
        state_ref[p] = jnp.exp(cum_end_col) * m0 + sm[P2:] + gh[:, P2:]
        y = y_s[:C] + y_s[C:]

        mean = head_sum(y) * (1.0 / RWKV_HEAD)
        d = y - mean
        var = head_sum(d * d) * (1.0 / RWKV_HEAD)
        yn = d * lax.rsqrt(var + GN_EPS) * lnw_ref[:, cs] + lnb_ref[:, cs]
        bonus = head_sum(r * k2 * rk_ref[:, cs]) * v
        o_ref[0, :, cs] = ((yn + bonus) * g_ref[0, :, cs]).astype(o_ref.dtype)


def wkv7(r, k, v, lw, a, g, v_first, v_gate, k_k, k_a, r_k, ln_w, ln_b, *, width=512):
    bsz, t, c = r.shape
    has_vmix = v_first is not None
    n_pairs = width // LANES
    seq_spec = pl.BlockSpec((1, WKV_CHUNK, width), lambda bi, hg, ci: (bi, ci, hg))
    ch_spec = pl.BlockSpec((1, width), lambda bi, hg, ci: (0, hg))
    seq_args = [r, k, v, lw, a, g] + ([v_first, v_gate] if has_vmix else [])
    ch_args = [k_k, k_a, r_k, ln_w, ln_b]
    return pl.pallas_call(
        functools.partial(_wkv_kernel, n_pairs=n_pairs, has_vmix=has_vmix),
        out_shape=jax.ShapeDtypeStruct((bsz, t, c), BF16),
        grid=(bsz, c // width, t // WKV_CHUNK),
        in_specs=[seq_spec] * len(seq_args) + [ch_spec] * len(ch_args),
        out_specs=seq_spec,
        scratch_shapes=[pltpu.VMEM((n_pairs, LANES, LANES), F32)],
        compiler_params=_params("parallel", "parallel", "arbitrary"),
        name="wkv7",
    )(*seq_args, *ch_args)


def _ffn_in_kernel(x_ref, wg_ref, wu_ref, cwg_ref, cwu_ref, cbg_ref, cbu_ref, o_ref,
                   wg_bf, wu_bf, carry_g, carry_u, *, tiles_per_seq):
    i = pl.program_id(1)

    @pl.when(i == 0)
    def _():
        wg_bf[...] = wg_ref[...].astype(BF16)
        wu_bf[...] = wu_ref[...].astype(BF16)

    @pl.when(i % tiles_per_seq == 0)
    def _():
        carry_g[...] = jnp.zeros_like(carry_g)
        carry_u[...] = jnp.zeros_like(carry_u)

    x = x_ref[...]
    tm = x.shape[0]

    def conv(u, carry_ref, cw_ref, cb_ref):
        row = lax.broadcasted_iota(jnp.int32, u.shape, 0)
        last1 = carry_ref[SUBLANES - 1:SUBLANES, :]
        last2 = carry_ref[SUBLANES - 2:SUBLANES - 1, :]
        u1 = jnp.where(row == 0, last1, pltpu.roll(u, 1, axis=0))
        u2 = jnp.where(row == 0, last2, jnp.where(row == 1, last1, pltpu.roll(u, 2, axis=0)))
        carry_ref[...] = u[tm - SUBLANES:, :]
        return cw_ref[0:1, :] * u + cw_ref[1:2, :] * u1 + cw_ref[2:3, :] * u2 + cb_ref[...]

    gate = conv(jnp.dot(x, wg_bf[...], preferred_element_type=F32), carry_g, cwg_ref, cbg_ref)
    up = conv(jnp.dot(x, wu_bf[...], preferred_element_type=F32), carry_u, cwu_ref, cbu_ref)
    o_ref[...] = (gate * jax.nn.sigmoid(gate) * up).astype(o_ref.dtype)


def ffn_in(h, w_in, conv_w, conv_b, *, seq, tm=1024, tn=256):
    m, k = h.shape
    f = w_in.shape[1] // 2
    nf = f // tn
    return pl.pallas_call(
        functools.partial(_ffn_in_kernel, tiles_per_seq=seq // tm),
        out_shape=jax.ShapeDtypeStruct((m, f), BF16),
        grid=(nf, m // tm),
        in_specs=[pl.BlockSpec((tm, k), lambda j, i: (i, 0)),
                  pl.BlockSpec((k, tn), lambda j, i: (0, j)),
                  pl.BlockSpec((k, tn), lambda j, i: (0, j + nf)),
                  pl.BlockSpec((CONV_WIDTH, tn), lambda j, i: (0, j)),
                  pl.BlockSpec((CONV_WIDTH, tn), lambda j, i: (0, j + nf)),
                  pl.BlockSpec((1, tn), lambda j, i: (0, j)),
                  pl.BlockSpec((1, tn), lambda j, i: (0, j + nf))],
        out_specs=pl.BlockSpec((tm, tn), lambda j, i: (i, j)),
        scratch_shapes=[pltpu.VMEM((k, tn), BF16), pltpu.VMEM((k, tn), BF16),
                        pltpu.VMEM((SUBLANES, tn), F32), pltpu.VMEM((SUBLANES, tn), F32)],
        compiler_params=_params("parallel", "arbitrary"),
        name="ffn_in",
    )(h, w_in, w_in, conv_w, conv_w, conv_b, conv_b)


def _attn_kernel(q_ref, kp_ref, kc_ref, vp_ref, vc_ref, o_ref, lse_ref, *, n_heads, dilation, span):
    n = pl.program_id(2)
    blk = ATTN_BLOCK
    qi = lax.broadcasted_iota(jnp.int32, (blk, 2 * blk), 0)
    kj = lax.broadcasted_iota(jnp.int32, (blk, 2 * blk), 1) - blk
    dist = qi - kj
    valid = jnp.logical_and(jnp.logical_and(dist >= 0, dist <= span),
                            jnp.logical_or(kj >= 0, n > 0))
    dist_f = (dist * dilation).astype(F32)
    for h in range(n_heads):
        sl = slice(h * HEAD_DIM, (h + 1) * HEAD_DIM)
        slope = 2.0 ** (-8.0 * (h + 1) / n_heads)
        q = q_ref[0, :, sl]
        keys = jnp.concatenate([kp_ref[0, :, sl], kc_ref[0, :, sl]], axis=0)
        vals = jnp.concatenate([vp_ref[0, :, sl], vc_ref[0, :, sl]], axis=0)
        s = _dot_nt(q, keys)
        s = jnp.where(valid, s - slope * dist_f, NEG_BIG)
        m = jnp.max(s, axis=-1, keepdims=True)
        p = jnp.exp(s - m)
        l = jnp.sum(p, axis=-1, keepdims=True)
        o = _dot(p, vals) / l
        o_ref[0, :, sl] = o.astype(o_ref.dtype)
        lse_ref[0, :, sl] = jnp.broadcast_to(m + jnp.log(l), (blk, HEAD_DIM))


def dilated_group_attention(q, k, v, group, n_groups, n_heads):
    bsz, s, qw = q.shape
    gw = n_heads * HEAD_DIM
    window, dil = DILATION_PAIRS[group]
    span = window // dil
    assert span <= ATTN_BLOCK
    length = s // dil
    nb = length // ATTN_BLOCK
    view = lambda t: t.reshape(bsz, length, dil * qw)
    blk = (1, ATTN_BLOCK, gw)
    cur = pl.BlockSpec(blk, lambda b, r, n: (b, n, r * n_groups + group))
    prev = pl.BlockSpec(blk, lambda b, r, n: (b, jnp.maximum(n - 1, 0), r * n_groups + group))
    out_sds = jax.ShapeDtypeStruct((bsz, length, dil * gw), F32)
    o, lse = pl.pallas_call(
        functools.partial(_attn_kernel, n_heads=n_heads, dilation=dil, span=span),
        out_shape=[out_sds, out_sds],
        grid=(bsz, dil, nb),
        in_specs=[cur, prev, cur, prev, cur],
        out_specs=[pl.BlockSpec(blk, lambda b, r, n: (b, n, r))] * 2,
        compiler_params=_params("parallel", "parallel", "parallel"),
        name="dilated_attn_g%d" % group,
    )(view(q), view(k), view(k), view(v), view(v))
    return o.reshape(bsz, s, gw), lse.reshape(bsz, s, gw)


def _merge_kernel(*refs):
    n = (len(refs) - 1) // 2
    o_refs, l_refs, out_ref = refs[:n], refs[n:2 * n], refs[-1]
    lses = [l[...] for l in l_refs]
    m = functools.reduce(jnp.maximum, lses)
    ws = [jnp.exp(l - m) for l in lses]
    tot = functools.reduce(lambda x, y: x + y, ws)
    acc = functools.reduce(lambda x, y: x + y, [w / tot * o[...] for w, o in zip(ws, o_refs)])
    out_ref[...] = acc.astype(out_ref.dtype)


def merge_groups(os_, lses, tm=256):
    m, n = os_[0].shape
    spec = pl.BlockSpec((tm, n), lambda i: (i, 0))
    return pl.pallas_call(
        _merge_kernel,
        out_shape=jax.ShapeDtypeStruct((m, n), BF16),
        grid=(m // tm,),
        in_specs=[spec] * (2 * len(os_)),
        out_specs=spec,
        compiler_params=_params("parallel"),
        name="merge_groups",
    )(*os_, *lses)


def _row(vec):
    return vec.reshape(1, -1)


def _tile(pref, dim):
    return min(pref, dim)


def rwkv_layer(x, a_mix, sh_mix, gate_mix, mu, w_r, w_k, w_v, w_o, w0, w1, w2, a0, a1, a2, g1, g2,
               k_k, k_a, r_k, ln_w, ln_b, v_first, v_mix):
    bsz, seq, d = x.shape
    m = bsz * seq
    tm, tn = _tile(1024, seq), _tile(512, d)
    bf = lambda w: w.astype(BF16)
    xr, xw, xk, xv, xa, xg = (t.reshape(m, d) for t in norm_token_mix(x, a_mix, sh_mix, mu))
    r, = matmul(xr, bf(w_r), _ep_plain, [F32], n=d, tm=tm, tn=tn)
    k, = matmul(xk, bf(w_k), _ep_plain, [F32], n=d, tm=tm, tn=tn)
    v, = matmul(xv, bf(w_v), _ep_plain, [F32], n=d, tm=tm, tn=tn)
    lw = lora(xw, bf(w1), bf(w2), _row(w0), act="tanh", post="log_decay")
    a = lora(xa, bf(a1), bf(a2), _row(a0), act="none", post="sigmoid")
    g = lora(xg, bf(g1), bf(g2), jnp.zeros((1, d), F32), act="sigmoid", post="none")
    if v_mix is None:
        v_first = v
        vf = v_gate = None
    else:
        v0, v1, v2 = v_mix
        v_gate = lora(xv, bf(v1), bf(v2), _row(v0), act="none", post="sigmoid")
        vf = v_first
    to3 = lambda t: None if t is None else t.reshape(bsz, seq, d)
    y = wkv7(to3(r), to3(k), to3(v), to3(lw), to3(a), to3(g), to3(vf), to3(v_gate),
             _row(k_k), _row(k_a), _row(r_k), _row(ln_w), _row(ln_b), width=_tile(512, d))
    x2, = matmul(y.reshape(m, d), bf(w_o), _ep_residual_gate, [F32], n=d, tm=tm, tn=tn,
                 extras=[(x.reshape(m, d), "tile"), (gate_mix, "batch")])
    return x2.reshape(bsz, seq, d), v_first


def attn_layer(x, a_mix, sh_mix, gate_mix, w_q, q_gain, w_o, kv_params, k_shared, v_shared):
    bsz, seq, d = x.shape
    m = bsz * seq
    q_width = w_q.shape[1]
    n_groups = q_gain.shape[0]
    group_w = q_width // n_groups
    n_heads = group_w // HEAD_DIM
    tm, tn = _tile(1024, seq), _tile(512, group_w)
    bf = lambda w: w.astype(BF16)
    if kv_params is not None:
        kv_norm, w_kv, k_gain = kv_params
        a_all = jnp.concatenate([a_mix, jnp.broadcast_to(kv_norm[None, None, :], (bsz, 1, d))], axis=1)
        b_all = jnp.concatenate([sh_mix, jnp.zeros((bsz, 1, d), F32)], axis=1)
        h, hkv = norm_affine(x, a_all, b_all)
        k_gain_row = jnp.tile(k_gain, (1, n_heads)).reshape(1, q_width)
        w_kv_bf = bf(w_kv)
        k_shared, = matmul(hkv.reshape(m, d), w_kv_bf, _ep_k_norm, [BF16], n=q_width, tm=tm, tn=tn,
                           extras=[(k_gain_row, "col")])
        v_shared, = matmul(hkv.reshape(m, d), w_kv_bf, _ep_plain, [BF16], n=q_width, col_off=q_width,
                           tm=tm, tn=tn)
        k_shared = k_shared.reshape(bsz, seq, q_width)
        v_shared = v_shared.reshape(bsz, seq, q_width)
    else:
        h, = norm_affine(x, a_mix, sh_mix)
    q_gain_row = jnp.tile(q_gain, (1, n_heads)).reshape(1, q_width)
    q, = matmul(h.reshape(m, d), bf(w_q), _ep_q_norm, [BF16], n=q_width, tm=tm, tn=tn,
                extras=[(q_gain_row, "col")])
    q = q.reshape(bsz, seq, q_width)
    os_, lses = [], []
    for grp in range(n_groups):
        o, lse = dilated_group_attention(q, k_shared, v_shared, grp, n_groups, n_heads)
        os_.append(o.reshape(m, group_w))
        lses.append(lse.reshape(m, group_w))
    merged = merge_groups(os_, lses)
    x2, = matmul(merged, bf(w_o), _ep_residual_gate, [F32], n=d, tm=tm, tn=_tile(512, d),
                 extras=[(x.reshape(m, d), "tile"), (gate_mix, "batch")])
    return x2.reshape(bsz, seq, d), k_shared, v_shared


def ffn_layer(x, a_ffn, sh_ffn, gate_ffn, w_in, conv_w, conv_b, w_out, ffn_tn=256, nk=2):
    bsz, seq, d = x.shape
    m = bsz * seq
    h, = norm_affine(x, a_ffn, sh_ffn)
    act = ffn_in(h.reshape(m, d), w_in, conv_w, _row(conv_b), seq=seq, tm=_tile(1024, seq), tn=ffn_tn)
    x2, = matmul(act, w_out.astype(BF16), _ep_residual_gate, [F32], n=d, tm=_tile(1024, seq),
                 tn=_tile(512, d), nk=nk, extras=[(x.reshape(m, d), "tile"), (gate_ffn, "batch")])
    return x2.reshape(bsz, seq, d)


def kernel(x, c, ada_w, ada_table, norm_mix, norm_ffn, rwkv_mu, rwkv_w_r, rwkv_w_k, rwkv_w_v, rwkv_w_o, rwkv_w0, rwkv_w1, rwkv_w2, rwkv_a0, rwkv_a1, rwkv_a2, rwkv_g1, rwkv_g2, rwkv_k_k, rwkv_k_a, rwkv_r_k, rwkv_ln_w, rwkv_ln_b, rwkv_v0, rwkv_v1, rwkv_v2, kv_norm, attn_w_kv, attn_k_gain, attn_w_q, attn_q_gain, attn_w_o, ffn_w_in, ffn_conv_w, ffn_conv_b, ffn_w_out):
    bsz, seq, d = x.shape
    depth = ada_table.shape[0]
    n_a = rwkv_w_r.shape[0]
    cond = ada_cond(c, ada_w.astype(BF16)).reshape(bsz, 6, d)
    v_first = None
    k_shared = v_shared = None
    for layer in range(depth):
        mod = cond + ada_table[layer][None]
        sh_mix, sc_mix, gate_mix, sh_ffn, sc_ffn, gate_ffn = (mod[:, i:i + 1, :] for i in range(6))
        a_mix = norm_mix[layer][None, None, :] * (1.0 + sc_mix)
        if layer < n_a:
            i = layer
            v_mix = None if i == 0 else (rwkv_v0[i - 1], rwkv_v1[i - 1], rwkv_v2[i - 1])
            x, v_first = rwkv_layer(
                x, a_mix, sh_mix, gate_mix, rwkv_mu[i], rwkv_w_r[i], rwkv_w_k[i], rwkv_w_v[i], rwkv_w_o[i],
                rwkv_w0[i], rwkv_w1[i], rwkv_w2[i], rwkv_a0[i], rwkv_a1[i], rwkv_a2[i], rwkv_g1[i], rwkv_g2[i],
                rwkv_k_k[i], rwkv_k_a[i], rwkv_r_k[i], rwkv_ln_w[i], rwkv_ln_b[i], v_first, v_mix)
        else:
            j = layer - n_a
            kv_params = (kv_norm, attn_w_kv, attn_k_gain) if j == 0 else None
            x, k_shared, v_shared = attn_layer(x, a_mix, sh_mix, gate_mix, attn_w_q[j], attn_q_gain[j],
                                               attn_w_o[j], kv_params, k_shared, v_shared)
        a_ffn = norm_ffn[layer][None, None, :] * (1.0 + sc_ffn)
        x = ffn_layer(x, a_ffn, sh_ffn, gate_ffn, ffn_w_in[layer], ffn_conv_w[layer], ffn_conv_b[layer],
                      ffn_w_out[layer])
    return x
```

```python
import functools
import math

import jax
import jax.numpy as jnp
from jax import lax
from jax.experimental import pallas as pl
from jax.experimental.pallas import tpu as pltpu

F32 = jnp.float32
BF16 = jnp.bfloat16

VMEM_LIMIT_BYTES = 56 * 1024 * 1024
LANES = 128
SUBLANES = 8

NORM_EPS = 1e-6
GN_EPS = 64e-5
RWKV_HEAD = 64
WKV_CHUNK = 64
HEAD_DIM = 128
ATTN_BLOCK = 128
DILATION_PAIRS = ((128, 1), (512, 4), (2048, 16))
CONV_WIDTH = 3
NEG_BIG = -1e30


def _params(*sem):
    return pltpu.CompilerParams(dimension_semantics=sem, vmem_limit_bytes=VMEM_LIMIT_BYTES)


def _dot(a, b):
    return jnp.dot(a.astype(BF16), b.astype(BF16), preferred_element_type=F32)


def _dot_nt(a, b):
    return lax.dot_general(a.astype(BF16), b.astype(BF16), (((1,), (1,)), ((), ())),
                           preferred_element_type=F32)


def _dot_tn(a, b):
    return lax.dot_general(a.astype(BF16), b.astype(BF16), (((0,), (0,)), ((), ())),
                           preferred_element_type=F32)


def _split_hi_lo(x):
    hi = x.astype(BF16)
    lo = (x - hi.astype(F32)).astype(BF16)
    return hi, lo


def _ada_kernel(c_ref, w_ref, o_ref):
    c = c_ref[...]
    o_ref[...] = _dot(c * jax.nn.sigmoid(c), w_ref[...])


def ada_cond(c, ada_w, tn=512):
    b, d = c.shape
    n = ada_w.shape[1]
    cp = jnp.pad(c, ((0, SUBLANES - b), (0, 0)))
    out = pl.pallas_call(
        _ada_kernel,
        out_shape=jax.ShapeDtypeStruct((SUBLANES, n), F32),
        grid=(n // tn,),
        in_specs=[pl.BlockSpec((SUBLANES, d), lambda j: (0, 0)),
                  pl.BlockSpec((d, tn), lambda j: (0, j))],
        out_specs=pl.BlockSpec((SUBLANES, tn), lambda j: (0, j)),
        compiler_params=_params("parallel"),
        name="ada_cond",
    )(cp, ada_w)
    return out[:b]


def _rms(x):
    return x * lax.rsqrt(jnp.mean(x * x, axis=-1, keepdims=True) + NORM_EPS)


def _norm_affine_kernel(x_ref, a_ref, b_ref, *o_refs):
    xn = _rms(x_ref[0])
    for k, o_ref in enumerate(o_refs):
        o_ref[0] = (xn * a_ref[0, k:k + 1, :] + b_ref[0, k:k + 1, :]).astype(o_ref.dtype)


def norm_affine(x, a, b, tm=256):
    bsz, s, d = x.shape
    n_out = a.shape[1]
    outs = pl.pallas_call(
        _norm_affine_kernel,
        out_shape=[jax.ShapeDtypeStruct((bsz, s, d), BF16)] * n_out,
        grid=(bsz, s // tm),
        in_specs=[pl.BlockSpec((1, tm, d), lambda bi, i: (bi, i, 0)),
                  pl.BlockSpec((1, n_out, d), lambda bi, i: (bi, 0, 0)),
                  pl.BlockSpec((1, n_out, d), lambda bi, i: (bi, 0, 0))],
        out_specs=[pl.BlockSpec((1, tm, d), lambda bi, i: (bi, i, 0))] * n_out,
        compiler_params=_params("parallel", "parallel"),
        name="norm_affine",
    )(x, a, b)
    return outs


def _norm_mix_kernel(x_ref, xp_ref, a_ref, b_ref, mu_ref, *o_refs):
    i = pl.program_id(1)
    a = a_ref[0]
    b = b_ref[0]
    h = _rms(x_ref[0]) * a + b
    hp = _rms(xp_ref[0]) * a + b
    prev_row = jnp.where(i > 0, hp[SUBLANES - 1:SUBLANES, :], 0.0)
    row = lax.broadcasted_iota(jnp.int32, h.shape, 0)
    h_prev = jnp.where(row == 0, prev_row, pltpu.roll(h, 1, axis=0))
    xx = h_prev - h
    for k, o_ref in enumerate(o_refs):
        o_ref[0] = (h + xx * mu_ref[k:k + 1, :]).astype(o_ref.dtype)


def norm_token_mix(x, a, b, mu, tm=256):
    bsz, s, d = x.shape
    n_out = mu.shape[0]
    rows_per_blk = tm // SUBLANES
    outs = pl.pallas_call(
        _norm_mix_kernel,
        out_shape=[jax.ShapeDtypeStruct((bsz, s, d), BF16)] * n_out,
        grid=(bsz, s // tm),
        in_specs=[pl.BlockSpec((1, tm, d), lambda bi, i: (bi, i, 0)),
                  pl.BlockSpec((1, SUBLANES, d),
                               lambda bi, i: (bi, jnp.maximum(i * rows_per_blk - 1, 0), 0)),
                  pl.BlockSpec((1, 1, d), lambda bi, i: (bi, 0, 0)),
                  pl.BlockSpec((1, 1, d), lambda bi, i: (bi, 0, 0)),
                  pl.BlockSpec((n_out, d), lambda bi, i: (0, 0))],
        out_specs=[pl.BlockSpec((1, tm, d), lambda bi, i: (bi, i, 0))] * n_out,
        compiler_params=_params("parallel", "parallel"),
        name="norm_mix",
    )(x, x, a, b, mu)
    return outs


def _mm_kernel(x_ref, w_ref, *rest, epilogue, n_extra, nk):
    extras = rest[:n_extra]
    if nk == 1:
        outs = rest[n_extra:]
        epilogue(_dot(x_ref[...], w_ref[...]), extras, outs)
        return
    outs = rest[n_extra:-1]
    acc_ref = rest[-1]
    kk = pl.program_id(2)
    part = _dot(x_ref[...], w_ref[...])

    @pl.when(kk == 0)
    def _():
        acc_ref[...] = part

    @pl.when(kk > 0)
    def _():
        acc_ref[...] += part

    @pl.when(kk == nk - 1)
    def _():
        epilogue(acc_ref[...], extras, outs)


def matmul(x, w, epilogue, out_dtypes, *, n, col_off=0, tm, tn, nk=1, extras=(), layer=None):
    m, k = x.shape
    tk = k // nk
    assert m % tm == 0 and n % tn == 0 and col_off % tn == 0 and k % nk == 0
    joff = col_off // tn
    if layer is None:
        w_spec = pl.BlockSpec((tk, tn), lambda i, j, kk: (kk, j + joff))
    else:
        w_spec = pl.BlockSpec((None, tk, tn), lambda i, j, kk: (layer, kk, j + joff))
    in_specs = [pl.BlockSpec((tm, tk), lambda i, j, kk: (i, kk)), w_spec]
    args = [x, w]
    for arr, kind in extras:
        if kind == "tile":
            in_specs.append(pl.BlockSpec((tm, tn), lambda i, j, kk: (i, j)))
        elif kind == "col":
            in_specs.append(pl.BlockSpec((1, tn), lambda i, j, kk: (0, j)))
        else:
            seq = m // arr.shape[0]
            in_specs.append(pl.BlockSpec((1, 1, tn), lambda i, j, kk, seq=seq: ((i * tm) // seq, 0, j)))
        args.append(arr)
    scratch = [] if nk == 1 else [pltpu.VMEM((tm, tn), F32)]
    outs = pl.pallas_call(
        functools.partial(_mm_kernel, epilogue=epilogue, n_extra=len(extras), nk=nk),
        out_shape=[jax.ShapeDtypeStruct((m, n), dt) for dt in out_dtypes],
        grid=(m // tm, n // tn, nk),
        in_specs=in_specs,
        out_specs=[pl.BlockSpec((tm, tn), lambda i, j, kk: (i, j))] * len(out_dtypes),
        scratch_shapes=scratch,
        compiler_params=_params("parallel", "parallel", "arbitrary"),
        name="matmul_" + epilogue.__name__.strip("_"),
    )(*args)
    return outs


def _ep_plain(acc, extras, outs):
    outs[0][...] = acc.astype(outs[0].dtype)


def _ep_residual_gate(acc, extras, outs):
    resid_ref, gate_ref = extras
    outs[0][...] = resid_ref[...] + gate_ref[0] * acc


def _ep_head_rms(acc, extras, outs, scale):
    gain_ref, = extras
    tn = acc.shape[1]
    for h in range(tn // HEAD_DIM):
        sl = slice(h * HEAD_DIM, (h + 1) * HEAD_DIM)
        blk = acc[:, sl]
        y = blk * lax.rsqrt(jnp.mean(blk * blk, axis=-1, keepdims=True) + NORM_EPS) * gain_ref[:, sl]
        if scale != 1.0:
            y = y * scale
        outs[0][:, sl] = y.astype(outs[0].dtype)


def _ep_q_norm(acc, extras, outs):
    _ep_head_rms(acc, extras, outs, HEAD_DIM ** -0.5)


def _ep_k_norm(acc, extras, outs):
    _ep_head_rms(acc, extras, outs, 1.0)


def _lora_kernel(x_ref, w1_ref, w2_ref, bias_ref, o_ref, *, act, post):
    t = _dot(x_ref[...], w1_ref[...])
    if act == "tanh":
        t = jnp.tanh(t)
    elif act == "sigmoid":
        t = jax.nn.sigmoid(t)
    z = bias_ref[...] + _dot(t, w2_ref[...])
    if post == "sigmoid":
        z = jax.nn.sigmoid(z)
    elif post == "log_decay":
        z = -math.exp(-0.5) * jax.nn.sigmoid(z)
    o_ref[...] = z.astype(o_ref.dtype)


def lora(x, w1, w2, bias, *, act, post, out_dtype=F32, tm=512):
    m, k = x.shape
    r = w1.shape[1]
    n = w2.shape[1]
    tm = min(tm, m)
    return pl.pallas_call(
        functools.partial(_lora_kernel, act=act, post=post),
        out_shape=jax.ShapeDtypeStruct((m, n), out_dtype),
        grid=(m // tm,),
        in_specs=[pl.BlockSpec((tm, k), lambda i: (i, 0)),
                  pl.BlockSpec((k, r), lambda i: (0, 0)),
                  pl.BlockSpec((r, n), lambda i: (0, 0)),
                  pl.BlockSpec((1, n), lambda i: (0, 0))],
        out_specs=pl.BlockSpec((tm, n), lambda i: (i, 0)),
        compiler_params=_params("parallel"),
        name="lora_" + act + "_" + post,
    )(x, w1, w2, bias)


def _wkv_kernel(*refs, n_pairs, has_vmix):
    if has_vmix:
        (r_ref, k_ref, v_ref, lw_ref, a_ref, g_ref, vf_ref, vg_ref,
         kk_ref, ka_ref, rk_ref, lnw_ref, lnb_ref, o_ref, state_ref) = refs
    else:
        (r_ref, k_ref, v_ref, lw_ref, a_ref, g_ref,
         kk_ref, ka_ref, rk_ref, lnw_ref, lnb_ref, o_ref, state_ref) = refs
    C = WKV_CHUNK
    P2 = 2 * C
    c_idx = pl.program_id(2)

    @pl.when(c_idx == 0)
    def _():
        state_ref[...] = jnp.zeros_like(state_ref)

    lane = lax.broadcasted_iota(jnp.int32, (C, LANES), 1)
    lo_half = lane < RWKV_HEAD
    ri = lax.broadcasted_iota(jnp.int32, (P2, P2), 0)
    ci = lax.broadcasted_iota(jnp.int32, (P2, P2), 1)
    same_head = (ri < C) == (ci < C)
    strict = jnp.logical_and(same_head, ci < ri)
    incl = jnp.logical_and(same_head, ci <= ri)
    block_ones = jnp.where(same_head, 1.0, 0.0).astype(BF16)
    tri_r = lax.broadcasted_iota(jnp.int32, (C, C), 0)
    tri_c = lax.broadcasted_iota(jnp.int32, (C, C), 1)
    tril_ones = jnp.where(tri_c <= tri_r, 1.0, 0.0).astype(BF16)
    ones_sq = jnp.ones((P2, P2), BF16)

    def stack(x):
        return jnp.concatenate([jnp.where(lo_half, x, 0.0), jnp.where(lo_half, 0.0, x)], axis=0)

    def head_sum(x):
        return jnp.dot(x.astype(BF16), block_ones, preferred_element_type=F32)

    pairs = range(n_pairs)
    cols = [slice(p * LANES, (p + 1) * LANES) for p in pairs]

    def each(fn, *lists):
        return [fn(*args) for args in zip(*lists)]

    r = [r_ref[0, :, cs] for cs in cols]
    k = [k_ref[0, :, cs] for cs in cols]
    v = [v_ref[0, :, cs] for cs in cols]
    lw = [lw_ref[0, :, cs] for cs in cols]
    a = [a_ref[0, :, cs] for cs in cols]
    if has_vmix:
        v = [vi + (vf_ref[0, :, cs] - vi) * vg_ref[0, :, cs] for vi, cs in zip(v, cols)]
    kkr = [ki * kk_ref[:, cs] for ki, cs in zip(k, cols)]
    nrm = each(lambda x: jnp.sqrt(head_sum(x * x)), kkr)
    kk = each(lambda x, n: x / jnp.maximum(n, 1e-12), kkr, nrm)
    k2 = [ki * (1.0 + (ai - 1.0) * ka_ref[:, cs]) for ki, ai, cs in zip(k, a, cols)]

    lw_split = each(_split_hi_lo, lw)
    cum = each(lambda hl: jnp.dot(tril_ones, hl[0], preferred_element_type=F32)
               + jnp.dot(tril_ones, hl[1], preferred_element_type=F32), lw_split)
    cum_end_col = each(lambda hl: lax.dot_general(jnp.concatenate(hl, axis=0), ones_sq,
                                                  (((0,), (0,)), ((), ())),
                                                  preferred_element_type=F32), lw_split)

    e_prev = each(lambda c, l: jnp.exp(c - l), cum, lw)
    e_cum = each(jnp.exp, cum)
    e_neg = each(lambda c: jnp.exp(-c), cum)
    e_tail = each(lambda c: jnp.exp(c[C - 1:C, :] - c), cum)
    b = each(lambda x, y: x * y, kk, a)
    a_t = each(lambda x, e: stack(-x * e), kk, e_prev)
    r_t = each(lambda x, e: stack(x * e), r, e_cum)
    b_t = each(lambda x, e: stack(x * e), b, e_neg)
    k_t = each(lambda x, e: stack(x * e), k2, e_neg)
    b_g = each(lambda x, e: stack(x * e), b, e_tail)
    k_g = each(lambda x, e: stack(x * e), k2, e_tail)
    v_s = each(stack, v)

    gram = each(lambda at, rt, bt, kt: _dot_nt(jnp.concatenate([at, rt], axis=0),
                                               jnp.concatenate([bt, kt], axis=0)), a_t, r_t, b_t, k_t)
    p_ab = each(lambda gm: jnp.where(strict, gm[:P2, :P2], 0.0), gram)
    p_ak = each(lambda gm: jnp.where(strict, gm[:P2, P2:], 0.0), gram)
    p_r = each(lambda gm: jnp.concatenate([jnp.where(incl, gm[P2:, :P2], 0.0),
                                           jnp.where(incl, gm[P2:, P2:], 0.0)], axis=1), gram)

    x = each(lambda at, pk, vs: jnp.concatenate([at, _dot(pk, vs)], axis=1), a_t, p_ak, v_s)
    pw = p_ab
    n_fac = int(math.log2(C))
    for i in range(n_fac):
        x = each(lambda xi, pi: xi + _dot(pi, xi), x, pw)
        if i + 1 < n_fac:
            pw = each(lambda pi: _dot(pi, pi), pw)

    rhs = each(lambda xi, vs: jnp.concatenate(
        [xi, jnp.concatenate([jnp.zeros_like(vs), vs], axis=1)], axis=0), x, v_s)
    ry = each(_dot, p_r, rhs)
    gh = each(lambda bg, kg, rh: _dot_tn(jnp.concatenate([bg, kg], axis=0), rh), b_g, k_g, rhs)

    m0 = [state_ref[p] for p in pairs]
    sm = each(lambda rt, ryi, ghi, m: _dot(jnp.concatenate([rt + ryi[:, :P2], ghi[:, :P2]], axis=0), m),
              r_t, ry, gh, m0)
    for p in pairs:
        state_ref[p] = jnp.exp(cum_end_col[p]) * m0[p] + sm[p][P2:] + gh[p][:, P2:]
    y_s = each(lambda s, ryi: s[:P2] + ryi[:, P2:], sm, ry)
    y = each(lambda ys: ys[:C] + ys[C:], y_s)

    mean = each(lambda yi: head_sum(yi) * (1.0 / RWKV_HEAD), y)
    d = each(lambda yi, mi: yi - mi, y, mean)
    var = each(lambda di: head_sum(di * di) * (1.0 / RWKV_HEAD), d)
    bonus_w = [head_sum(ri * ki * rk_ref[:, cs]) for ri, ki, cs in zip(r, k2, cols)]
    for p, cs in zip(pairs, cols):
        yn = d[p] * lax.rsqrt(var[p] + GN_EPS) * lnw_ref[:, cs] + lnb_ref[:, cs]
        o_ref[0, :, cs] = ((yn + bonus_w[p] * v[p]) * g_ref[0, :, cs]).astype(o_ref.dtype)


def wkv7(r, k, v, lw, a, g, v_first, v_gate, k_k, k_a, r_k, ln_w, ln_b, *, width=512):
    bsz, t, c = r.shape
    has_vmix = v_first is not None
    n_pairs = width // LANES
    seq_spec = pl.BlockSpec((1, WKV_CHUNK, width), lambda bi, hg, ci: (bi, ci, hg))
    ch_spec = pl.BlockSpec((1, width), lambda bi, hg, ci: (0, hg))
    seq_args = [r, k, v, lw, a, g] + ([v_first, v_gate] if has_vmix else [])
    ch_args = [k_k, k_a, r_k, ln_w, ln_b]
    return pl.pallas_call(
        functools.partial(_wkv_kernel, n_pairs=n_pairs, has_vmix=has_vmix),
        out_shape=jax.ShapeDtypeStruct((bsz, t, c), BF16),
        grid=(bsz, c // width, t // WKV_CHUNK),
        in_specs=[seq_spec] * len(seq_args) + [ch_spec] * len(ch_args),
        out_specs=seq_spec,
        scratch_shapes=[pltpu.VMEM((n_pairs, LANES, LANES), F32)],
        compiler_params=_params("parallel", "parallel", "arbitrary"),
        name="wkv7",
    )(*seq_args, *ch_args)


def _ffn_in_kernel(x_ref, wg_ref, wu_ref, cwg_ref, cwu_ref, cbg_ref, cbu_ref, o_ref,
                   wg_bf, wu_bf, carry_g, carry_u, *, tiles_per_seq):
    i = pl.program_id(1)

    @pl.when(i == 0)
    def _():
        wg_bf[...] = wg_ref[...].astype(BF16)
        wu_bf[...] = wu_ref[...].astype(BF16)

    @pl.when(i % tiles_per_seq == 0)
    def _():
        carry_g[...] = jnp.zeros_like(carry_g)
        carry_u[...] = jnp.zeros_like(carry_u)

    x = x_ref[...]
    tm = x.shape[0]

    def conv(u, carry_ref, cw_ref, cb_ref):
        row = lax.broadcasted_iota(jnp.int32, u.shape, 0)
        last1 = carry_ref[SUBLANES - 1:SUBLANES, :]
        last2 = carry_ref[SUBLANES - 2:SUBLANES - 1, :]
        u1 = jnp.where(row == 0, last1, pltpu.roll(u, 1, axis=0))
        u2 = jnp.where(row == 0, last2, jnp.where(row == 1, last1, pltpu.roll(u, 2, axis=0)))
        carry_ref[...] = u[tm - SUBLANES:, :]
        return cw_ref[0:1, :] * u + cw_ref[1:2, :] * u1 + cw_ref[2:3, :] * u2 + cb_ref[...]

    gate = conv(jnp.dot(x, wg_bf[...], preferred_element_type=F32), carry_g, cwg_ref, cbg_ref)
    up = conv(jnp.dot(x, wu_bf[...], preferred_element_type=F32), carry_u, cwu_ref, cbu_ref)
    o_ref[...] = (gate * jax.nn.sigmoid(gate) * up).astype(o_ref.dtype)


def ffn_in(h, w_in, conv_w, conv_b, layer, *, seq, tm=1024, tn=256):
    m, k = h.shape
    f = w_in.shape[2] // 2
    nf = f // tn
    return pl.pallas_call(
        functools.partial(_ffn_in_kernel, tiles_per_seq=seq // tm),
        out_shape=jax.ShapeDtypeStruct((m, f), BF16),
        grid=(nf, m // tm),
        in_specs=[pl.BlockSpec((tm, k), lambda j, i: (i, 0)),
                  pl.BlockSpec((None, k, tn), lambda j, i: (layer, 0, j)),
                  pl.BlockSpec((None, k, tn), lambda j, i: (layer, 0, j + nf)),
                  pl.BlockSpec((None, CONV_WIDTH, tn), lambda j, i: (layer, 0, j)),
                  pl.BlockSpec((None, CONV_WIDTH, tn), lambda j, i: (layer, 0, j + nf)),
                  pl.BlockSpec((1, tn), lambda j, i: (0, j)),
                  pl.BlockSpec((1, tn), lambda j, i: (0, j + nf))],
        out_specs=pl.BlockSpec((tm, tn), lambda j, i: (i, j)),
        scratch_shapes=[pltpu.VMEM((k, tn), BF16), pltpu.VMEM((k, tn), BF16),
                        pltpu.VMEM((SUBLANES, tn), F32), pltpu.VMEM((SUBLANES, tn), F32)],
        compiler_params=_params("parallel", "arbitrary"),
        name="ffn_in",
    )(h, w_in, w_in, conv_w, conv_w, conv_b, conv_b)


def _attn_kernel(q_ref, kp_ref, kc_ref, vp_ref, vc_ref, o_ref, lse_ref, *, n_heads, dilation, span):
    n = pl.program_id(2)
    blk = ATTN_BLOCK
    qi = lax.broadcasted_iota(jnp.int32, (blk, 2 * blk), 0)
    kj = lax.broadcasted_iota(jnp.int32, (blk, 2 * blk), 1) - blk
    dist = qi - kj
    valid = jnp.logical_and(jnp.logical_and(dist >= 0, dist <= span),
                            jnp.logical_or(kj >= 0, n > 0))
    dist_f = (dist * dilation).astype(F32)
    for h in range(n_heads):
        sl = slice(h * HEAD_DIM, (h + 1) * HEAD_DIM)
        slope = 2.0 ** (-8.0 * (h + 1) / n_heads)
        q = q_ref[0, :, sl]
        keys = jnp.concatenate([kp_ref[0, :, sl], kc_ref[0, :, sl]], axis=0)
        vals = jnp.concatenate([vp_ref[0, :, sl], vc_ref[0, :, sl]], axis=0)
        s = _dot_nt(q, keys)
        s = jnp.where(valid, s - slope * dist_f, NEG_BIG)
        m = jnp.max(s, axis=-1, keepdims=True)
        p = jnp.exp(s - m)
        l = jnp.sum(p, axis=-1, keepdims=True)
        o = _dot(p, vals) / l
        o_ref[0, :, sl] = o.astype(o_ref.dtype)
        lse_ref[0, :, sl] = jnp.broadcast_to(m + jnp.log(l), (blk, HEAD_DIM))


def dilated_group_attention(q, k, v, group, n_groups, n_heads):
    bsz, s, qw = q.shape
    gw = n_heads * HEAD_DIM
    window, dil = DILATION_PAIRS[group]
    span = window // dil
    assert span <= ATTN_BLOCK
    length = s // dil
    nb = length // ATTN_BLOCK
    view = lambda t: t.reshape(bsz, length, dil * qw)
    blk = (1, ATTN_BLOCK, gw)
    cur = pl.BlockSpec(blk, lambda b, r, n: (b, n, r * n_groups + group))
    prev = pl.BlockSpec(blk, lambda b, r, n: (b, jnp.maximum(n - 1, 0), r * n_groups + group))
    out_sds = jax.ShapeDtypeStruct((bsz, length, dil * gw), F32)
    o, lse = pl.pallas_call(
        functools.partial(_attn_kernel, n_heads=n_heads, dilation=dil, span=span),
        out_shape=[out_sds, out_sds],
        grid=(bsz, dil, nb),
        in_specs=[cur, prev, cur, prev, cur],
        out_specs=[pl.BlockSpec(blk, lambda b, r, n: (b, n, r))] * 2,
        compiler_params=_params("parallel", "parallel", "parallel"),
        name="dilated_attn_g%d" % group,
    )(view(q), view(k), view(k), view(v), view(v))
    return o.reshape(bsz, s, gw), lse.reshape(bsz, s, gw)


def _merge_kernel(*refs):
    n = (len(refs) - 1) // 2
    o_refs, l_refs, out_ref = refs[:n], refs[n:2 * n], refs[-1]
    lses = [l[...] for l in l_refs]
    m = functools.reduce(jnp.maximum, lses)
    ws = [jnp.exp(l - m) for l in lses]
    tot = functools.reduce(lambda x, y: x + y, ws)
    acc = functools.reduce(lambda x, y: x + y, [w / tot * o[...] for w, o in zip(ws, o_refs)])
    out_ref[...] = acc.astype(out_ref.dtype)


def merge_groups(os_, lses, tm=256):
    m, n = os_[0].shape
    spec = pl.BlockSpec((tm, n), lambda i: (i, 0))
    return pl.pallas_call(
        _merge_kernel,
        out_shape=jax.ShapeDtypeStruct((m, n), BF16),
        grid=(m // tm,),
        in_specs=[spec] * (2 * len(os_)),
        out_specs=spec,
        compiler_params=_params("parallel"),
        name="merge_groups",
    )(*os_, *lses)


def _row(vec):
    return vec.reshape(1, -1)


def _tile(pref, dim):
    return min(pref, dim)


def rwkv_layer(x, a_mix, sh_mix, gate_mix, mu, w_r, w_k, w_v, w_o, idx, w0, w1, w2, a0, a1, a2, g1, g2,
               k_k, k_a, r_k, ln_w, ln_b, v_first, v_mix):
    bsz, seq, d = x.shape
    m = bsz * seq
    tm, tn = _tile(1024, seq), _tile(512, d)
    bf = lambda w: w.astype(BF16)
    xr, xw, xk, xv, xa, xg = (t.reshape(m, d) for t in norm_token_mix(x, a_mix, sh_mix, mu))
    r, = matmul(xr, w_r, _ep_plain, [F32], n=d, tm=tm, tn=tn, layer=idx)
    k, = matmul(xk, w_k, _ep_plain, [F32], n=d, tm=tm, tn=tn, layer=idx)
    v, = matmul(xv, w_v, _ep_plain, [F32], n=d, tm=tm, tn=tn, layer=idx)
    lw = lora(xw, bf(w1), bf(w2), _row(w0), act="tanh", post="log_decay")
    a = lora(xa, bf(a1), bf(a2), _row(a0), act="none", post="sigmoid")
    g = lora(xg, bf(g1), bf(g2), jnp.zeros((1, d), F32), act="sigmoid", post="none")
    if v_mix is None:
        v_first = v
        vf = v_gate = None
    else:
        v0, v1, v2 = v_mix
        v_gate = lora(xv, bf(v1), bf(v2), _row(v0), act="none", post="sigmoid")
        vf = v_first
    to3 = lambda t: None if t is None else t.reshape(bsz, seq, d)
    y = wkv7(to3(r), to3(k), to3(v), to3(lw), to3(a), to3(g), to3(vf), to3(v_gate),
             _row(k_k), _row(k_a), _row(r_k), _row(ln_w), _row(ln_b), width=_tile(1024, d))
    x2, = matmul(y.reshape(m, d), w_o, _ep_residual_gate, [F32], n=d, tm=tm, tn=tn, layer=idx,
                 extras=[(x.reshape(m, d), "tile"), (gate_mix, "batch")])
    return x2.reshape(bsz, seq, d), v_first


def attn_layer(x, a_mix, sh_mix, gate_mix, w_q, w_o, idx, q_gain, kv_params, k_shared, v_shared):
    bsz, seq, d = x.shape
    m = bsz * seq
    q_width = w_q.shape[2]
    n_groups = q_gain.shape[0]
    group_w = q_width // n_groups
    n_heads = group_w // HEAD_DIM
    tm, tn = _tile(1024, seq), _tile(512, group_w)
    bf = lambda w: w.astype(BF16)
    if kv_params is not None:
        kv_norm, w_kv, k_gain = kv_params
        a_all = jnp.concatenate([a_mix, jnp.broadcast_to(kv_norm[None, None, :], (bsz, 1, d))], axis=1)
        b_all = jnp.concatenate([sh_mix, jnp.zeros((bsz, 1, d), F32)], axis=1)
        h, hkv = norm_affine(x, a_all, b_all)
        k_gain_row = jnp.tile(k_gain, (1, n_heads)).reshape(1, q_width)
        k_shared, = matmul(hkv.reshape(m, d), w_kv, _ep_k_norm, [BF16], n=q_width, tm=tm, tn=tn,
                           extras=[(k_gain_row, "col")])
        v_shared, = matmul(hkv.reshape(m, d), w_kv, _ep_plain, [BF16], n=q_width, col_off=q_width,
                           tm=tm, tn=tn)
        k_shared = k_shared.reshape(bsz, seq, q_width)
        v_shared = v_shared.reshape(bsz, seq, q_width)
    else:
        h, = norm_affine(x, a_mix, sh_mix)
    q_gain_row = jnp.tile(q_gain, (1, n_heads)).reshape(1, q_width)
    q, = matmul(h.reshape(m, d), w_q, _ep_q_norm, [BF16], n=q_width, tm=tm, tn=tn, layer=idx,
                extras=[(q_gain_row, "col")])
    q = q.reshape(bsz, seq, q_width)
    os_, lses = [], []
    for grp in range(n_groups):
        o, lse = dilated_group_attention(q, k_shared, v_shared, grp, n_groups, n_heads)
        os_.append(o.reshape(m, group_w))
        lses.append(lse.reshape(m, group_w))
    merged = merge_groups(os_, lses)
    x2, = matmul(merged, w_o, _ep_residual_gate, [F32], n=d, tm=tm, tn=_tile(512, d), layer=idx,
                 extras=[(x.reshape(m, d), "tile"), (gate_mix, "batch")])
    return x2.reshape(bsz, seq, d), k_shared, v_shared


def ffn_layer(x, a_ffn, sh_ffn, gate_ffn, w_in_all, conv_w_all, layer, conv_b, w_out, ffn_tn=256, nk=2):
    bsz, seq, d = x.shape
    m = bsz * seq
    h, = norm_affine(x, a_ffn, sh_ffn)
    act = ffn_in(h.reshape(m, d), w_in_all, conv_w_all, _row(conv_b), layer, seq=seq,
                 tm=_tile(1024, seq), tn=ffn_tn)
    x2, = matmul(act, w_out.astype(BF16), _ep_residual_gate, [F32], n=d, tm=_tile(1024, seq),
                 tn=_tile(512, d), nk=nk, extras=[(x.reshape(m, d), "tile"), (gate_ffn, "batch")])
    return x2.reshape(bsz, seq, d)


def kernel(x, c, ada_w, ada_table, norm_mix, norm_ffn, rwkv_mu, rwkv_w_r, rwkv_w_k, rwkv_w_v, rwkv_w_o, rwkv_w0, rwkv_w1, rwkv_w2, rwkv_a0, rwkv_a1, rwkv_a2, rwkv_g1, rwkv_g2, rwkv_k_k, rwkv_k_a, rwkv_r_k, rwkv_ln_w, rwkv_ln_b, rwkv_v0, rwkv_v1, rwkv_v2, kv_norm, attn_w_kv, attn_k_gain, attn_w_q, attn_q_gain, attn_w_o, ffn_w_in, ffn_conv_w, ffn_conv_b, ffn_w_out):
    bsz, seq, d = x.shape
    depth = ada_table.shape[0]
    n_a = rwkv_w_r.shape[0]
    cond = ada_cond(c, ada_w).reshape(bsz, 6, d)
    v_first = None
    k_shared = v_shared = None
    for layer in range(depth):
        mod = cond + ada_table[layer][None]
        sh_mix, sc_mix, gate_mix, sh_ffn, sc_ffn, gate_ffn = (mod[:, i:i + 1, :] for i in range(6))
        a_mix = norm_mix[layer][None, None, :] * (1.0 + sc_mix)
        if layer < n_a:
            i = layer
            v_mix = None if i == 0 else (rwkv_v0[i - 1], rwkv_v1[i - 1], rwkv_v2[i - 1])
            x, v_first = rwkv_layer(
                x, a_mix, sh_mix, gate_mix, rwkv_mu[i], rwkv_w_r, rwkv_w_k, rwkv_w_v, rwkv_w_o, i,
                rwkv_w0[i], rwkv_w1[i], rwkv_w2[i], rwkv_a0[i], rwkv_a1[i], rwkv_a2[i], rwkv_g1[i], rwkv_g2[i],
                rwkv_k_k[i], rwkv_k_a[i], rwkv_r_k[i], rwkv_ln_w[i], rwkv_ln_b[i], v_first, v_mix)
        else:
            j = layer - n_a
            kv_params = (kv_norm, attn_w_kv, attn_k_gain) if j == 0 else None
            x, k_shared, v_shared = attn_layer(x, a_mix, sh_mix, gate_mix, attn_w_q, attn_w_o, j,
                                               attn_q_gain[j], kv_params, k_shared, v_shared)
        a_ffn = norm_ffn[layer][None, None, :] * (1.0 + sc_ffn)
        x = ffn_layer(x, a_ffn, sh_ffn, gate_ffn, ffn_w_in, ffn_conv_w, layer, ffn_conv_b[layer],
                      ffn_w_out[layer])
    return x
```

```python
import functools
import math

import jax
import jax.numpy as jnp
from jax import lax
from jax.experimental import pallas as pl
from jax.experimental.pallas import tpu as pltpu

F32 = jnp.float32
BF16 = jnp.bfloat16

VMEM_LIMIT_BYTES = 56 * 1024 * 1024
LANES = 128
SUBLANES = 8

NORM_EPS = 1e-6
GN_EPS = 64e-5
RWKV_HEAD = 64
WKV_CHUNK = 64
HEAD_DIM = 128
ATTN_BLOCK = 128
DILATION_PAIRS = ((128, 1), (512, 4), (2048, 16))
CONV_WIDTH = 3
NEG_BIG = -1e30


def _params(*sem):
    return pltpu.CompilerParams(dimension_semantics=sem, vmem_limit_bytes=VMEM_LIMIT_BYTES)


def _dot(a, b):
    return jnp.dot(a.astype(BF16), b.astype(BF16), preferred_element_type=F32)


def _dot_nt(a, b):
    return lax.dot_general(a.astype(BF16), b.astype(BF16), (((1,), (1,)), ((), ())),
                           preferred_element_type=F32)


def _dot_tn(a, b):
    return lax.dot_general(a.astype(BF16), b.astype(BF16), (((0,), (0,)), ((), ())),
                           preferred_element_type=F32)


def _split_hi_lo(x):
    hi = x.astype(BF16)
    lo = (x - hi.astype(F32)).astype(BF16)
    return hi, lo


def _ada_kernel(c_ref, w_ref, o_ref):
    c = c_ref[...]
    o_ref[...] = _dot(c * jax.nn.sigmoid(c), w_ref[...])


def ada_cond(c, ada_w, tn=512):
    b, d = c.shape
    n = ada_w.shape[1]
    cp = jnp.pad(c, ((0, SUBLANES - b), (0, 0)))
    out = pl.pallas_call(
        _ada_kernel,
        out_shape=jax.ShapeDtypeStruct((SUBLANES, n), F32),
        grid=(n // tn,),
        in_specs=[pl.BlockSpec((SUBLANES, d), lambda j: (0, 0)),
                  pl.BlockSpec((d, tn), lambda j: (0, j))],
        out_specs=pl.BlockSpec((SUBLANES, tn), lambda j: (0, j)),
        compiler_params=_params("parallel"),
        name="ada_cond",
    )(cp, ada_w)
    return out[:b]


def _rms(x):
    return x * lax.rsqrt(jnp.mean(x * x, axis=-1, keepdims=True) + NORM_EPS)


def _norm_affine_kernel(x_ref, a_ref, b_ref, *o_refs):
    xn = _rms(x_ref[0])
    for k, o_ref in enumerate(o_refs):
        o_ref[0] = (xn * a_ref[0, k:k + 1, :] + b_ref[0, k:k + 1, :]).astype(o_ref.dtype)


def norm_affine(x, a, b, tm=256):
    bsz, s, d = x.shape
    n_out = a.shape[1]
    outs = pl.pallas_call(
        _norm_affine_kernel,
        out_shape=[jax.ShapeDtypeStruct((bsz, s, d), BF16)] * n_out,
        grid=(bsz, s // tm),
        in_specs=[pl.BlockSpec((1, tm, d), lambda bi, i: (bi, i, 0)),
                  pl.BlockSpec((1, n_out, d), lambda bi, i: (bi, 0, 0)),
                  pl.BlockSpec((1, n_out, d), lambda bi, i: (bi, 0, 0))],
        out_specs=[pl.BlockSpec((1, tm, d), lambda bi, i: (bi, i, 0))] * n_out,
        compiler_params=_params("parallel", "parallel"),
        name="norm_affine",
    )(x, a, b)
    return outs


def _norm_mix_kernel(x_ref, xp_ref, a_ref, b_ref, mu_ref, *o_refs):
    i = pl.program_id(1)
    a = a_ref[0]
    b = b_ref[0]
    h = _rms(x_ref[0]) * a + b
    hp = _rms(xp_ref[0]) * a + b
    prev_row = jnp.where(i > 0, hp[SUBLANES - 1:SUBLANES, :], 0.0)
    row = lax.broadcasted_iota(jnp.int32, h.shape, 0)
    h_prev = jnp.where(row == 0, prev_row, pltpu.roll(h, 1, axis=0))
    xx = h_prev - h
    for k, o_ref in enumerate(o_refs):
        o_ref[0] = (h + xx * mu_ref[k:k + 1, :]).astype(o_ref.dtype)


def norm_token_mix(x, a, b, mu, tm=256):
    bsz, s, d = x.shape
    n_out = mu.shape[0]
    rows_per_blk = tm // SUBLANES
    outs = pl.pallas_call(
        _norm_mix_kernel,
        out_shape=[jax.ShapeDtypeStruct((bsz, s, d), BF16)] * n_out,
        grid=(bsz, s // tm),
        in_specs=[pl.BlockSpec((1, tm, d), lambda bi, i: (bi, i, 0)),
                  pl.BlockSpec((1, SUBLANES, d),
                               lambda bi, i: (bi, jnp.maximum(i * rows_per_blk - 1, 0), 0)),
                  pl.BlockSpec((1, 1, d), lambda bi, i: (bi, 0, 0)),
                  pl.BlockSpec((1, 1, d), lambda bi, i: (bi, 0, 0)),
                  pl.BlockSpec((n_out, d), lambda bi, i: (0, 0))],
        out_specs=[pl.BlockSpec((1, tm, d), lambda bi, i: (bi, i, 0))] * n_out,
        compiler_params=_params("parallel", "parallel"),
        name="norm_mix",
    )(x, x, a, b, mu)
    return outs


def _mm_kernel(x_ref, w_ref, *rest, epilogue, n_extra, nk):
    extras = rest[:n_extra]
    if nk == 1:
        outs = rest[n_extra:]
        epilogue(_dot(x_ref[...], w_ref[...]), extras, outs)
        return
    outs = rest[n_extra:-1]
    acc_ref = rest[-1]
    kk = pl.program_id(2)
    part = _dot(x_ref[...], w_ref[...])

    @pl.when(kk == 0)
    def _():
        acc_ref[...] = part

    @pl.when(kk > 0)
    def _():
        acc_ref[...] += part

    @pl.when(kk == nk - 1)
    def _():
        epilogue(acc_ref[...], extras, outs)


def matmul(x, w, epilogue, out_dtypes, *, n, col_off=0, tm, tn, nk=1, extras=(), layer=None):
    m, k = x.shape
    tk = k // nk
    assert m % tm == 0 and n % tn == 0 and col_off % tn == 0 and k % nk == 0
    joff = col_off // tn
    if layer is None:
        w_spec = pl.BlockSpec((tk, tn), lambda i, j, kk: (kk, j + joff))
    else:
        w_spec = pl.BlockSpec((None, tk, tn), lambda i, j, kk: (layer, kk, j + joff))
    in_specs = [pl.BlockSpec((tm, tk), lambda i, j, kk: (i, kk)), w_spec]
    args = [x, w]
    for arr, kind in extras:
        if kind == "tile":
            in_specs.append(pl.BlockSpec((tm, tn), lambda i, j, kk: (i, j)))
        elif kind == "col":
            in_specs.append(pl.BlockSpec((1, tn), lambda i, j, kk: (0, j)))
        else:
            seq = m // arr.shape[0]
            in_specs.append(pl.BlockSpec((1, 1, tn), lambda i, j, kk, seq=seq: ((i * tm) // seq, 0, j)))
        args.append(arr)
    scratch = [] if nk == 1 else [pltpu.VMEM((tm, tn), F32)]
    outs = pl.pallas_call(
        functools.partial(_mm_kernel, epilogue=epilogue, n_extra=len(extras), nk=nk),
        out_shape=[jax.ShapeDtypeStruct((m, n), dt) for dt in out_dtypes],
        grid=(m // tm, n // tn, nk),
        in_specs=in_specs,
        out_specs=[pl.BlockSpec((tm, tn), lambda i, j, kk: (i, j))] * len(out_dtypes),
        scratch_shapes=scratch,
        compiler_params=_params("parallel", "parallel", "arbitrary"),
        name="matmul_" + epilogue.__name__.strip("_"),
    )(*args)
    return outs


def _ep_plain(acc, extras, outs):
    outs[0][...] = acc.astype(outs[0].dtype)


def _ep_residual_gate(acc, extras, outs):
    resid_ref, gate_ref = extras
    outs[0][...] = resid_ref[...] + gate_ref[0] * acc


def _ep_head_rms(acc, extras, outs, scale):
    gain_ref, = extras
    tn = acc.shape[1]
    for h in range(tn // HEAD_DIM):
        sl = slice(h * HEAD_DIM, (h + 1) * HEAD_DIM)
        blk = acc[:, sl]
        y = blk * lax.rsqrt(jnp.mean(blk * blk, axis=-1, keepdims=True) + NORM_EPS) * gain_ref[:, sl]
        if scale != 1.0:
            y = y * scale
        outs[0][:, sl] = y.astype(outs[0].dtype)


def _ep_q_norm(acc, extras, outs):
    _ep_head_rms(acc, extras, outs, HEAD_DIM ** -0.5)


def _ep_k_norm(acc, extras, outs):
    _ep_head_rms(acc, extras, outs, 1.0)


def _lora_kernel(x_ref, w1_ref, w2_ref, bias_ref, o_ref, *, act, post):
    t = _dot(x_ref[...], w1_ref[...])
    if act == "tanh":
        t = jnp.tanh(t)
    elif act == "sigmoid":
        t = jax.nn.sigmoid(t)
    z = bias_ref[...] + _dot(t, w2_ref[...])
    if post == "sigmoid":
        z = jax.nn.sigmoid(z)
    elif post == "log_decay":
        z = -math.exp(-0.5) * jax.nn.sigmoid(z)
    o_ref[...] = z.astype(o_ref.dtype)


def lora(x, w1, w2, bias, *, act, post, out_dtype=F32, tm=512):
    m, k = x.shape
    r = w1.shape[1]
    n = w2.shape[1]
    tm = min(tm, m)
    return pl.pallas_call(
        functools.partial(_lora_kernel, act=act, post=post),
        out_shape=jax.ShapeDtypeStruct((m, n), out_dtype),
        grid=(m // tm,),
        in_specs=[pl.BlockSpec((tm, k), lambda i: (i, 0)),
                  pl.BlockSpec((k, r), lambda i: (0, 0)),
                  pl.BlockSpec((r, n), lambda i: (0, 0)),
                  pl.BlockSpec((1, n), lambda i: (0, 0))],
        out_specs=pl.BlockSpec((tm, n), lambda i: (i, 0)),
        compiler_params=_params("parallel"),
        name="lora_" + act + "_" + post,
    )(x, w1, w2, bias)


def _wkv_kernel(*refs, n_pairs, has_vmix):
    if has_vmix:
        (r_ref, k_ref, v_ref, lw_ref, a_ref, g_ref, vf_ref, vg_ref,
         kk_ref, ka_ref, rk_ref, lnw_ref, lnb_ref, o_ref, state_ref) = refs
    else:
        (r_ref, k_ref, v_ref, lw_ref, a_ref, g_ref,
         kk_ref, ka_ref, rk_ref, lnw_ref, lnb_ref, o_ref, state_ref) = refs
    C = WKV_CHUNK
    P2 = 2 * C
    c_idx = pl.program_id(2)

    @pl.when(c_idx == 0)
    def _():
        state_ref[...] = jnp.zeros_like(state_ref)

    lane = lax.broadcasted_iota(jnp.int32, (C, LANES), 1)
    lo_half = lane < RWKV_HEAD
    ri = lax.broadcasted_iota(jnp.int32, (P2, P2), 0)
    ci = lax.broadcasted_iota(jnp.int32, (P2, P2), 1)
    same_head = (ri < C) == (ci < C)
    strict = jnp.logical_and(same_head, ci < ri)
    incl = jnp.logical_and(same_head, ci <= ri)
    block_ones = jnp.where(same_head, 1.0, 0.0).astype(BF16)
    tri_r = lax.broadcasted_iota(jnp.int32, (C, C), 0)
    tri_c = lax.broadcasted_iota(jnp.int32, (C, C), 1)
    tril_ones = jnp.where(tri_c <= tri_r, 1.0, 0.0).astype(BF16)
    ones_sq = jnp.ones((P2, P2), BF16)

    def stack(x):
        return jnp.concatenate([jnp.where(lo_half, x, 0.0), jnp.where(lo_half, 0.0, x)], axis=0)

    def head_sum(x):
        return jnp.dot(x.astype(BF16), block_ones, preferred_element_type=F32)

    pairs = range(n_pairs)
    cols = [slice(p * LANES, (p + 1) * LANES) for p in pairs]

    def each(fn, *lists):
        return [fn(*args) for args in zip(*lists)]

    r = [r_ref[0, :, cs] for cs in cols]
    k = [k_ref[0, :, cs] for cs in cols]
    v = [v_ref[0, :, cs] for cs in cols]
    lw = [lw_ref[0, :, cs] for cs in cols]
    a = [a_ref[0, :, cs] for cs in cols]
    if has_vmix:
        v = [vi + (vf_ref[0, :, cs] - vi) * vg_ref[0, :, cs] for vi, cs in zip(v, cols)]
    kkr = [ki * kk_ref[:, cs] for ki, cs in zip(k, cols)]
    nrm = each(lambda x: jnp.sqrt(head_sum(x * x)), kkr)
    kk = each(lambda x, n: x / jnp.maximum(n, 1e-12), kkr, nrm)
    k2 = [ki * (1.0 + (ai - 1.0) * ka_ref[:, cs]) for ki, ai, cs in zip(k, a, cols)]

    lw_split = each(_split_hi_lo, lw)
    cum = each(lambda hl: jnp.dot(tril_ones, hl[0], preferred_element_type=F32)
               + jnp.dot(tril_ones, hl[1], preferred_element_type=F32), lw_split)
    cum_end_col = each(lambda hl: lax.dot_general(jnp.concatenate(hl, axis=0), ones_sq,
                                                  (((0,), (0,)), ((), ())),
                                                  preferred_element_type=F32), lw_split)

    e_prev = each(lambda c, l: jnp.exp(c - l), cum, lw)
    e_cum = each(jnp.exp, cum)
    e_neg = each(lambda c: jnp.exp(-c), cum)
    e_tail = each(lambda c: jnp.exp(c[C - 1:C, :] - c), cum)
    b = each(lambda x, y: x * y, kk, a)
    a_t = each(lambda x, e: stack(-x * e), kk, e_prev)
    r_t = each(lambda x, e: stack(x * e), r, e_cum)
    b_t = each(lambda x, e: stack(x * e), b, e_neg)
    k_t = each(lambda x, e: stack(x * e), k2, e_neg)
    b_g = each(lambda x, e: stack(x * e), b, e_tail)
    k_g = each(lambda x, e: stack(x * e), k2, e_tail)
    v_s = each(stack, v)

    gram = each(lambda at, rt, bt, kt: _dot_nt(jnp.concatenate([at, rt], axis=0),
                                               jnp.concatenate([bt, kt], axis=0)), a_t, r_t, b_t, k_t)
    p_ab = each(lambda gm: jnp.where(strict, gm[:P2, :P2], 0.0), gram)
    p_ak = each(lambda gm: jnp.where(strict, gm[:P2, P2:], 0.0), gram)
    p_r = each(lambda gm: jnp.concatenate([jnp.where(incl, gm[P2:, :P2], 0.0),
                                           jnp.where(incl, gm[P2:, P2:], 0.0)], axis=1), gram)

    x = each(lambda at, pk, vs: jnp.concatenate([at, _dot(pk, vs)], axis=1), a_t, p_ak, v_s)
    pw = p_ab
    n_fac = int(math.log2(C))
    for i in range(n_fac):
        x = each(lambda xi, pi: xi + _dot(pi, xi), x, pw)
        if i + 1 < n_fac:
            pw = each(lambda pi: _dot(pi, pi), pw)

    rhs = each(lambda xi, vs: jnp.concatenate(
        [xi, jnp.concatenate([jnp.zeros_like(vs), vs], axis=1)], axis=0), x, v_s)
    ry = each(_dot, p_r, rhs)
    gh = each(lambda bg, kg, rh: _dot_tn(jnp.concatenate([bg, kg], axis=0), rh), b_g, k_g, rhs)

    m0 = [state_ref[p] for p in pairs]
    sm = each(lambda rt, ryi, ghi, m: _dot(jnp.concatenate([rt + ryi[:, :P2], ghi[:, :P2]], axis=0), m),
              r_t, ry, gh, m0)
    for p in pairs:
        state_ref[p] = jnp.exp(cum_end_col[p]) * m0[p] + sm[p][P2:] + gh[p][:, P2:]
    y_s = each(lambda s, ryi: s[:P2] + ryi[:, P2:], sm, ry)
    y = each(lambda ys: ys[:C] + ys[C:], y_s)

    mean = each(lambda yi: head_sum(yi) * (1.0 / RWKV_HEAD), y)
    d = each(lambda yi, mi: yi - mi, y, mean)
    var = each(lambda di: head_sum(di * di) * (1.0 / RWKV_HEAD), d)
    bonus_w = [head_sum(ri * ki * rk_ref[:, cs]) for ri, ki, cs in zip(r, k2, cols)]
    for p, cs in zip(pairs, cols):
        yn = d[p] * lax.rsqrt(var[p] + GN_EPS) * lnw_ref[:, cs] + lnb_ref[:, cs]
        o_ref[0, :, cs] = ((yn + bonus_w[p] * v[p]) * g_ref[0, :, cs]).astype(o_ref.dtype)


def wkv7(r, k, v, lw, a, g, v_first, v_gate, k_k, k_a, r_k, ln_w, ln_b, *, width=512):
    bsz, t, c = r.shape
    has_vmix = v_first is not None
    n_pairs = width // LANES
    seq_spec = pl.BlockSpec((1, WKV_CHUNK, width), lambda bi, hg, ci: (bi, ci, hg))
    ch_spec = pl.BlockSpec((1, width), lambda bi, hg, ci: (0, hg))
    seq_args = [r, k, v, lw, a, g] + ([v_first, v_gate] if has_vmix else [])
    ch_args = [k_k, k_a, r_k, ln_w, ln_b]
    return pl.pallas_call(
        functools.partial(_wkv_kernel, n_pairs=n_pairs, has_vmix=has_vmix),
        out_shape=jax.ShapeDtypeStruct((bsz, t, c), BF16),
        grid=(bsz, c // width, t // WKV_CHUNK),
        in_specs=[seq_spec] * len(seq_args) + [ch_spec] * len(ch_args),
        out_specs=seq_spec,
        scratch_shapes=[pltpu.VMEM((n_pairs, LANES, LANES), F32)],
        compiler_params=_params("parallel", "parallel", "arbitrary"),
        name="wkv7",
    )(*seq_args, *ch_args)


def _ffn_in_kernel(x_ref, wg_ref, wu_ref, cwg_ref, cwu_ref, cbg_ref, cbu_ref, o_ref,
                   wg_bf, wu_bf, carry_g, carry_u, *, tiles_per_seq):
    i = pl.program_id(1)

    @pl.when(i == 0)
    def _():
        wg_bf[...] = wg_ref[...].astype(BF16)
        wu_bf[...] = wu_ref[...].astype(BF16)

    @pl.when(i % tiles_per_seq == 0)
    def _():
        carry_g[...] = jnp.zeros_like(carry_g)
        carry_u[...] = jnp.zeros_like(carry_u)

    x = x_ref[...]
    tm = x.shape[0]

    def conv(u, carry_ref, cw_ref, cb_ref):
        row = lax.broadcasted_iota(jnp.int32, u.shape, 0)
        last1 = carry_ref[SUBLANES - 1:SUBLANES, :]
        last2 = carry_ref[SUBLANES - 2:SUBLANES - 1, :]
        u1 = jnp.where(row == 0, last1, pltpu.roll(u, 1, axis=0))
        u2 = jnp.where(row == 0, last2, jnp.where(row == 1, last1, pltpu.roll(u, 2, axis=0)))
        carry_ref[...] = u[tm - SUBLANES:, :]
        return cw_ref[0:1, :] * u + cw_ref[1:2, :] * u1 + cw_ref[2:3, :] * u2 + cb_ref[...]

    gate = conv(jnp.dot(x, wg_bf[...], preferred_element_type=F32), carry_g, cwg_ref, cbg_ref)
    up = conv(jnp.dot(x, wu_bf[...], preferred_element_type=F32), carry_u, cwu_ref, cbu_ref)
    o_ref[...] = (gate * jax.nn.sigmoid(gate) * up).astype(o_ref.dtype)


def ffn_in(h, w_in, conv_w, conv_b, layer, *, seq, tm=1024, tn=256):
    m, k = h.shape
    f = w_in.shape[2] // 2
    nf = f // tn
    return pl.pallas_call(
        functools.partial(_ffn_in_kernel, tiles_per_seq=seq // tm),
        out_shape=jax.ShapeDtypeStruct((m, f), BF16),
        grid=(nf, m // tm),
        in_specs=[pl.BlockSpec((tm, k), lambda j, i: (i, 0)),
                  pl.BlockSpec((None, k, tn), lambda j, i: (layer, 0, j)),
                  pl.BlockSpec((None, k, tn), lambda j, i: (layer, 0, j + nf)),
                  pl.BlockSpec((None, CONV_WIDTH, tn), lambda j, i: (layer, 0, j)),
                  pl.BlockSpec((None, CONV_WIDTH, tn), lambda j, i: (layer, 0, j + nf)),
                  pl.BlockSpec((1, tn), lambda j, i: (0, j)),
                  pl.BlockSpec((1, tn), lambda j, i: (0, j + nf))],
        out_specs=pl.BlockSpec((tm, tn), lambda j, i: (i, j)),
        scratch_shapes=[pltpu.VMEM((k, tn), BF16), pltpu.VMEM((k, tn), BF16),
                        pltpu.VMEM((SUBLANES, tn), F32), pltpu.VMEM((SUBLANES, tn), F32)],
        compiler_params=_params("parallel", "arbitrary"),
        name="ffn_in",
    )(h, w_in, w_in, conv_w, conv_w, conv_b, conv_b)


def _attn_kernel(*refs, n_groups, heads_per_block, seq, jobs_per_call):
    q_refs = refs[0:n_groups]
    k_refs = refs[n_groups:2 * n_groups]
    v_refs = refs[2 * n_groups:3 * n_groups]
    slope_ref, o_ref, qf, kf, vf, m_s, l_s, acc_s = refs[3 * n_groups:]
    blk = ATTN_BLOCK
    qi = lax.broadcasted_iota(jnp.int32, (blk, 2 * blk), 0)
    kj = lax.broadcasted_iota(jnp.int32, (blk, 2 * blk), 1) - blk
    dist = qi - kj

    for g in range(n_groups):
        window, dil = DILATION_PAIRS[g]
        span = window // dil
        nb = seq // (dil * blk)
        in_window = jnp.logical_and(dist >= 0, dist <= span)
        bias = (dist * dil).astype(F32)
        for h in range(heads_per_block):
            hs = slice(h * HEAD_DIM, (h + 1) * HEAD_DIM)
            qf[h] = q_refs[g][0, :, hs].astype(F32)
            kf[h] = k_refs[g][0, :, hs].astype(F32)
            vf[h] = v_refs[g][0, :, hs].astype(F32)

        def process(jobs, g=g, dil=dil, in_window=in_window, bias=bias):
            step = dil * blk
            chains = []
            for n, r in jobs:
                if isinstance(n, int):
                    base, base_prev = n * step, max(n - 1, 0) * step
                else:
                    base = pl.multiple_of(n * step, step)
                    base_prev = pl.multiple_of(jnp.maximum(n - 1, 0) * step, step)
                cur = pl.ds(base + r, blk, stride=dil)
                prev = pl.ds(base_prev + r, blk, stride=dil)
                valid = jnp.logical_and(in_window, jnp.logical_or(kj >= 0, n > 0))
                chains += [(h, cur, prev, valid) for h in range(heads_per_block)]
            s = [_dot_nt(qf[h, cur, :], jnp.concatenate([kf[h, prev, :], kf[h, cur, :]], axis=0))
                 for h, cur, prev, _ in chains]
            s = [jnp.where(valid, si - slope_ref[0:1, h * HEAD_DIM:h * HEAD_DIM + 1] * bias, NEG_BIG)
                 for si, (h, _, _, valid) in zip(s, chains)]
            m_blk = [jnp.max(si, axis=-1, keepdims=True) for si in s]
            p = [jnp.exp(si - mi) for si, mi in zip(s, m_blk)]
            l_blk = [jnp.broadcast_to(jnp.sum(pi, axis=-1, keepdims=True), (blk, HEAD_DIM)) for pi in p]
            o_blk = [_dot(pi, jnp.concatenate([vf[h, prev, :], vf[h, cur, :]], axis=0))
                     for pi, (h, cur, prev, _) in zip(p, chains)]
            m_blk = [jnp.broadcast_to(mi, (blk, HEAD_DIM)) for mi in m_blk]
            for (h, cur, _, _), mi, li, oi in zip(chains, m_blk, l_blk, o_blk):
                if g == 0:
                    m_s[h, cur, :] = mi
                    l_s[h, cur, :] = li
                    acc_s[h, cur, :] = oi
                else:
                    m_old = m_s[h, cur, :]
                    m_new = jnp.maximum(m_old, mi)
                    a_old = jnp.exp(m_old - m_new)
                    a_blk = jnp.exp(mi - m_new)
                    m_s[h, cur, :] = m_new
                    l_s[h, cur, :] = a_old * l_s[h, cur, :] + a_blk * li
                    acc_s[h, cur, :] = a_old * acc_s[h, cur, :] + a_blk * oi

        if nb == 1:
            for r0 in range(0, dil, jobs_per_call):
                process([(0, r) for r in range(r0, min(r0 + jobs_per_call, dil))])
        elif dil >= jobs_per_call:
            for r0 in range(0, dil, jobs_per_call):
                def body(n, carry, r0=r0):
                    process([(n, r) for r in range(r0, r0 + jobs_per_call)])
                    return carry
                lax.fori_loop(0, nb, body, 0)
        else:
            for r in range(dil):
                def body(n2, carry, r=r):
                    process([(n2 * jobs_per_call + j, r) for j in range(jobs_per_call)])
                    return carry
                lax.fori_loop(0, nb // jobs_per_call, body, 0)

    for h in range(heads_per_block):
        o_ref[0, :, h * HEAD_DIM:(h + 1) * HEAD_DIM] = (acc_s[h] / l_s[h]).astype(o_ref.dtype)


def dilated_attention(q, k, v, n_groups, n_heads, heads_per_block=2, jobs_per_call=4):
    bsz, s, qw = q.shape
    gw = n_heads * HEAD_DIM
    w = heads_per_block * HEAD_DIM
    n_hb = n_heads // heads_per_block
    for window, dil in DILATION_PAIRS[:n_groups]:
        assert window // dil <= ATTN_BLOCK and s % (dil * ATTN_BLOCK) == 0
    slopes = jnp.exp2(-8.0 * jnp.arange(1, n_heads + 1, dtype=F32) / n_heads)
    slope_row = jnp.repeat(slopes, HEAD_DIM).reshape(1, gw)
    specs = [pl.BlockSpec((1, s, w), lambda b, hb, g=g: (b, 0, g * n_hb + hb)) for g in range(n_groups)]
    scratch = pltpu.VMEM((heads_per_block, s, HEAD_DIM), F32)
    return pl.pallas_call(
        functools.partial(_attn_kernel, n_groups=n_groups, heads_per_block=heads_per_block, seq=s,
                          jobs_per_call=jobs_per_call),
        out_shape=jax.ShapeDtypeStruct((bsz, s, gw), BF16),
        grid=(bsz, n_hb),
        in_specs=specs * 3 + [pl.BlockSpec((1, w), lambda b, hb: (0, hb))],
        out_specs=pl.BlockSpec((1, s, w), lambda b, hb: (b, 0, hb)),
        scratch_shapes=[scratch] * 6,
        compiler_params=_params("parallel", "parallel"),
        name="dilated_attn",
    )(*([q] * n_groups + [k] * n_groups + [v] * n_groups), slope_row)


def _row(vec):
    return vec.reshape(1, -1)


def _tile(pref, dim):
    return min(pref, dim)


def rwkv_layer(x, a_mix, sh_mix, gate_mix, mu, w_r, w_k, w_v, w_o, idx, w0, w1, w2, a0, a1, a2, g1, g2,
               k_k, k_a, r_k, ln_w, ln_b, v_first, v_mix):
    bsz, seq, d = x.shape
    m = bsz * seq
    tm, tn = _tile(1024, seq), _tile(512, d)
    bf = lambda w: w.astype(BF16)
    xr, xw, xk, xv, xa, xg = (t.reshape(m, d) for t in norm_token_mix(x, a_mix, sh_mix, mu))
    r, = matmul(xr, w_r, _ep_plain, [F32], n=d, tm=tm, tn=tn, layer=idx)
    k, = matmul(xk, w_k, _ep_plain, [F32], n=d, tm=tm, tn=tn, layer=idx)
    v, = matmul(xv, w_v, _ep_plain, [F32], n=d, tm=tm, tn=tn, layer=idx)
    lw = lora(xw, bf(w1), bf(w2), _row(w0), act="tanh", post="log_decay")
    a = lora(xa, bf(a1), bf(a2), _row(a0), act="none", post="sigmoid")
    g = lora(xg, bf(g1), bf(g2), jnp.zeros((1, d), F32), act="sigmoid", post="none")
    if v_mix is None:
        v_first = v
        vf = v_gate = None
    else:
        v0, v1, v2 = v_mix
        v_gate = lora(xv, bf(v1), bf(v2), _row(v0), act="none", post="sigmoid")
        vf = v_first
    to3 = lambda t: None if t is None else t.reshape(bsz, seq, d)
    y = wkv7(to3(r), to3(k), to3(v), to3(lw), to3(a), to3(g), to3(vf), to3(v_gate),
             _row(k_k), _row(k_a), _row(r_k), _row(ln_w), _row(ln_b), width=_tile(1024, d))
    x2, = matmul(y.reshape(m, d), w_o, _ep_residual_gate, [F32], n=d, tm=tm, tn=tn, layer=idx,
                 extras=[(x.reshape(m, d), "tile"), (gate_mix, "batch")])
    return x2.reshape(bsz, seq, d), v_first


def attn_layer(x, a_mix, sh_mix, gate_mix, w_q, w_o, idx, q_gain, kv_params, k_shared, v_shared):
    bsz, seq, d = x.shape
    m = bsz * seq
    q_width = w_q.shape[2]
    n_groups = q_gain.shape[0]
    group_w = q_width // n_groups
    n_heads = group_w // HEAD_DIM
    tm, tn = _tile(1024, seq), _tile(512, group_w)
    bf = lambda w: w.astype(BF16)
    if kv_params is not None:
        kv_norm, w_kv, k_gain = kv_params
        a_all = jnp.concatenate([a_mix, jnp.broadcast_to(kv_norm[None, None, :], (bsz, 1, d))], axis=1)
        b_all = jnp.concatenate([sh_mix, jnp.zeros((bsz, 1, d), F32)], axis=1)
        h, hkv = norm_affine(x, a_all, b_all)
        k_gain_row = jnp.tile(k_gain, (1, n_heads)).reshape(1, q_width)
        k_shared, = matmul(hkv.reshape(m, d), w_kv, _ep_k_norm, [BF16], n=q_width, tm=tm, tn=tn,
                           extras=[(k_gain_row, "col")])
        v_shared, = matmul(hkv.reshape(m, d), w_kv, _ep_plain, [BF16], n=q_width, col_off=q_width,
                           tm=tm, tn=tn)
        k_shared = k_shared.reshape(bsz, seq, q_width)
        v_shared = v_shared.reshape(bsz, seq, q_width)
    else:
        h, = norm_affine(x, a_mix, sh_mix)
    q_gain_row = jnp.tile(q_gain, (1, n_heads)).reshape(1, q_width)
    q, = matmul(h.reshape(m, d), w_q, _ep_q_norm, [BF16], n=q_width, tm=tm, tn=tn, layer=idx,
                extras=[(q_gain_row, "col")])
    q = q.reshape(bsz, seq, q_width)
    merged = dilated_attention(q, k_shared, v_shared, n_groups, n_heads).reshape(m, group_w)
    x2, = matmul(merged, w_o, _ep_residual_gate, [F32], n=d, tm=tm, tn=_tile(512, d), layer=idx,
                 extras=[(x.reshape(m, d), "tile"), (gate_mix, "batch")])
    return x2.reshape(bsz, seq, d), k_shared, v_shared


def ffn_layer(x, a_ffn, sh_ffn, gate_ffn, w_in_all, conv_w_all, layer, conv_b, w_out, ffn_tn=256, nk=2):
    bsz, seq, d = x.shape
    m = bsz * seq
    h, = norm_affine(x, a_ffn, sh_ffn)
    act = ffn_in(h.reshape(m, d), w_in_all, conv_w_all, _row(conv_b), layer, seq=seq,
                 tm=_tile(1024, seq), tn=ffn_tn)
    x2, = matmul(act, w_out.astype(BF16), _ep_residual_gate, [F32], n=d, tm=_tile(1024, seq),
                 tn=_tile(512, d), nk=nk, extras=[(x.reshape(m, d), "tile"), (gate_ffn, "batch")])
    return x2.reshape(bsz, seq, d)


def kernel(x, c, ada_w, ada_table, norm_mix, norm_ffn, rwkv_mu, rwkv_w_r, rwkv_w_k, rwkv_w_v, rwkv_w_o, rwkv_w0, rwkv_w1, rwkv_w2, rwkv_a0, rwkv_a1, rwkv_a2, rwkv_g1, rwkv_g2, rwkv_k_k, rwkv_k_a, rwkv_r_k, rwkv_ln_w, rwkv_ln_b, rwkv_v0, rwkv_v1, rwkv_v2, kv_norm, attn_w_kv, attn_k_gain, attn_w_q, attn_q_gain, attn_w_o, ffn_w_in, ffn_conv_w, ffn_conv_b, ffn_w_out):
    bsz, seq, d = x.shape
    depth = ada_table.shape[0]
    n_a = rwkv_w_r.shape[0]
    cond = ada_cond(c, ada_w).reshape(bsz, 6, d)
    v_first = None
    k_shared = v_shared = None
    for layer in range(depth):
        mod = cond + ada_table[layer][None]
        sh_mix, sc_mix, gate_mix, sh_ffn, sc_ffn, gate_ffn = (mod[:, i:i + 1, :] for i in range(6))
        a_mix = norm_mix[layer][None, None, :] * (1.0 + sc_mix)
        if layer < n_a:
            i = layer
            v_mix = None if i == 0 else (rwkv_v0[i - 1], rwkv_v1[i - 1], rwkv_v2[i - 1])
            x, v_first = rwkv_layer(
                x, a_mix, sh_mix, gate_mix, rwkv_mu[i], rwkv_w_r, rwkv_w_k, rwkv_w_v, rwkv_w_o, i,
                rwkv_w0[i], rwkv_w1[i], rwkv_w2[i], rwkv_a0[i], rwkv_a1[i], rwkv_a2[i], rwkv_g1[i], rwkv_g2[i],
                rwkv_k_k[i], rwkv_k_a[i], rwkv_r_k[i], rwkv_ln_w[i], rwkv_ln_b[i], v_first, v_mix)
        else:
            j = layer - n_a
            kv_params = (kv_norm, attn_w_kv, attn_k_gain) if j == 0 else None
            x, k_shared, v_shared = attn_layer(x, a_mix, sh_mix, gate_mix, attn_w_q, attn_w_o, j,
                                               attn_q_gain[j], kv_params, k_shared, v_shared)
        a_ffn = norm_ffn[layer][None, None, :] * (1.0 + sc_ffn)
        x = ffn_layer(x, a_ffn, sh_ffn, gate_ffn, ffn_w_in, ffn_conv_w, layer, ffn_conv_b[layer],
                      ffn_w_out[layer])
    return x
```

```python
import functools
import math

import jax
import jax.numpy as jnp
from jax import lax
from jax.experimental import pallas as pl
from jax.experimental.pallas import tpu as pltpu

F32 = jnp.float32
BF16 = jnp.bfloat16

VMEM_LIMIT_BYTES = 56 * 1024 * 1024
LANES = 128
SUBLANES = 8

NORM_EPS = 1e-6
GN_EPS = 64e-5
RWKV_HEAD = 64
WKV_CHUNK = 64
HEAD_DIM = 128
ATTN_BLOCK = 128
DILATION_PAIRS = ((128, 1), (512, 4), (2048, 16))
CONV_WIDTH = 3
NEG_BIG = -1e30


def _params(*sem):
    return pltpu.CompilerParams(dimension_semantics=sem, vmem_limit_bytes=VMEM_LIMIT_BYTES)


def _dot(a, b):
    return jnp.dot(a.astype(BF16), b.astype(BF16), preferred_element_type=F32)


def _dot_nt(a, b):
    return lax.dot_general(a.astype(BF16), b.astype(BF16), (((1,), (1,)), ((), ())),
                           preferred_element_type=F32)


def _dot_tn(a, b):
    return lax.dot_general(a.astype(BF16), b.astype(BF16), (((0,), (0,)), ((), ())),
                           preferred_element_type=F32)


def _split_hi_lo(x):
    hi = x.astype(BF16)
    lo = (x - hi.astype(F32)).astype(BF16)
    return hi, lo


def _ada_kernel(c_ref, w_ref, o_ref):
    c = c_ref[...]
    o_ref[...] = _dot(c * jax.nn.sigmoid(c), w_ref[...])


def ada_cond(c, ada_w, tn=512):
    b, d = c.shape
    n = ada_w.shape[1]
    cp = jnp.pad(c, ((0, SUBLANES - b), (0, 0)))
    out = pl.pallas_call(
        _ada_kernel,
        out_shape=jax.ShapeDtypeStruct((SUBLANES, n), F32),
        grid=(n // tn,),
        in_specs=[pl.BlockSpec((SUBLANES, d), lambda j: (0, 0)),
                  pl.BlockSpec((d, tn), lambda j: (0, j))],
        out_specs=pl.BlockSpec((SUBLANES, tn), lambda j: (0, j)),
        compiler_params=_params("parallel"),
        name="ada_cond",
    )(cp, ada_w)
    return out[:b]


NORM_ROWS = 16
NORM_COLS = 512


def _inv_rms(load_block, d):
    ss = None
    for j in range(d // NORM_COLS):
        xb = load_block(slice(j * NORM_COLS, (j + 1) * NORM_COLS))
        sq = xb * xb
        ss = sq if ss is None else ss + sq
    return lax.rsqrt(jnp.sum(ss, axis=-1, keepdims=True) * (1.0 / d) + NORM_EPS)


def _norm_affine_kernel(x_ref, a_ref, b_ref, *o_refs):
    tm, d = x_ref.shape[1], x_ref.shape[2]

    def chunk(c, carry):
        rows = pl.ds(pl.multiple_of(c * NORM_ROWS, NORM_ROWS), NORM_ROWS)
        inv = _inv_rms(lambda cols: x_ref[0, rows, cols], d)
        for j in range(d // NORM_COLS):
            cols = slice(j * NORM_COLS, (j + 1) * NORM_COLS)
            xn = x_ref[0, rows, cols] * inv
            for k, o_ref in enumerate(o_refs):
                o_ref[0, rows, cols] = (xn * a_ref[0, k:k + 1, cols] + b_ref[0, k:k + 1, cols]).astype(o_ref.dtype)
        return carry

    lax.fori_loop(0, tm // NORM_ROWS, chunk, 0)


def norm_affine(x, a, b, tm=256):
    bsz, s, d = x.shape
    n_out = a.shape[1]
    outs = pl.pallas_call(
        _norm_affine_kernel,
        out_shape=[jax.ShapeDtypeStruct((bsz, s, d), BF16)] * n_out,
        grid=(bsz, s // tm),
        in_specs=[pl.BlockSpec((1, tm, d), lambda bi, i: (bi, i, 0)),
                  pl.BlockSpec((1, n_out, d), lambda bi, i: (bi, 0, 0)),
                  pl.BlockSpec((1, n_out, d), lambda bi, i: (bi, 0, 0))],
        out_specs=[pl.BlockSpec((1, tm, d), lambda bi, i: (bi, i, 0))] * n_out,
        compiler_params=_params("parallel", "parallel"),
        name="norm_affine",
    )(x, a, b)
    return outs


def _norm_mix_kernel(x_ref, xp_ref, a_ref, b_ref, mu_ref, *o_refs):
    i = pl.program_id(1)
    tm, d = x_ref.shape[1], x_ref.shape[2]
    row = lax.broadcasted_iota(jnp.int32, (NORM_ROWS, NORM_COLS), 0)

    def chunk(c, carry):
        r0 = pl.multiple_of(c * NORM_ROWS, NORM_ROWS)
        rows = pl.ds(r0, NORM_ROWS)
        before = pl.ds(pl.multiple_of(jnp.maximum(r0 - SUBLANES, 0), SUBLANES), SUBLANES)

        def shifted(cols):
            prev8 = jnp.where(c > 0, x_ref[0, before, cols], xp_ref[0, :, cols])
            return jnp.where(row == 0, prev8[SUBLANES - 1:SUBLANES, :],
                             pltpu.roll(x_ref[0, rows, cols], 1, axis=0))

        inv = _inv_rms(lambda cols: x_ref[0, rows, cols], d)
        inv_s = _inv_rms(shifted, d)
        seq_first = jnp.logical_and(jnp.logical_and(c == 0, i == 0), row == 0)
        for j in range(d // NORM_COLS):
            cols = slice(j * NORM_COLS, (j + 1) * NORM_COLS)
            a = a_ref[0, :, cols]
            b = b_ref[0, :, cols]
            h = x_ref[0, rows, cols] * inv * a + b
            h_prev = jnp.where(seq_first, 0.0, shifted(cols) * inv_s * a + b)
            xx = h_prev - h
            for k, o_ref in enumerate(o_refs):
                o_ref[0, rows, cols] = (h + xx * mu_ref[k:k + 1, cols]).astype(o_ref.dtype)
        return carry

    lax.fori_loop(0, tm // NORM_ROWS, chunk, 0)


def norm_token_mix(x, a, b, mu, tm=256):
    bsz, s, d = x.shape
    n_out = mu.shape[0]
    rows_per_blk = tm // SUBLANES
    outs = pl.pallas_call(
        _norm_mix_kernel,
        out_shape=[jax.ShapeDtypeStruct((bsz, s, d), BF16)] * n_out,
        grid=(bsz, s // tm),
        in_specs=[pl.BlockSpec((1, tm, d), lambda bi, i: (bi, i, 0)),
                  pl.BlockSpec((1, SUBLANES, d),
                               lambda bi, i: (bi, jnp.maximum(i * rows_per_blk - 1, 0), 0)),
                  pl.BlockSpec((1, 1, d), lambda bi, i: (bi, 0, 0)),
                  pl.BlockSpec((1, 1, d), lambda bi, i: (bi, 0, 0)),
                  pl.BlockSpec((n_out, d), lambda bi, i: (0, 0))],
        out_specs=[pl.BlockSpec((1, tm, d), lambda bi, i: (bi, i, 0))] * n_out,
        compiler_params=_params("parallel", "parallel"),
        name="norm_mix",
    )(x, x, a, b, mu)
    return outs


def _mm_kernel(x_ref, w_ref, *rest, epilogue, n_extra, nk):
    extras = rest[:n_extra]
    if nk == 1:
        outs = rest[n_extra:]
        epilogue(_dot(x_ref[...], w_ref[...]), extras, outs)
        return
    outs = rest[n_extra:-1]
    acc_ref = rest[-1]
    kk = pl.program_id(2)
    part = _dot(x_ref[...], w_ref[...])

    @pl.when(kk == 0)
    def _():
        acc_ref[...] = part

    @pl.when(kk > 0)
    def _():
        acc_ref[...] += part

    @pl.when(kk == nk - 1)
    def _():
        epilogue(acc_ref[...], extras, outs)


def matmul(x, w, epilogue, out_dtypes, *, n, col_off=0, tm, tn, nk=1, extras=(), layer=None):
    m, k = x.shape
    tk = k // nk
    assert m % tm == 0 and n % tn == 0 and col_off % tn == 0 and k % nk == 0
    joff = col_off // tn
    if layer is None:
        w_spec = pl.BlockSpec((tk, tn), lambda i, j, kk: (kk, j + joff))
    else:
        w_spec = pl.BlockSpec((None, tk, tn), lambda i, j, kk: (layer, kk, j + joff))
    in_specs = [pl.BlockSpec((tm, tk), lambda i, j, kk: (i, kk)), w_spec]
    args = [x, w]
    for arr, kind in extras:
        if kind == "tile":
            in_specs.append(pl.BlockSpec((tm, tn), lambda i, j, kk: (i, j)))
        elif kind == "col":
            in_specs.append(pl.BlockSpec((1, tn), lambda i, j, kk: (0, j)))
        else:
            seq = m // arr.shape[0]
            in_specs.append(pl.BlockSpec((1, 1, tn), lambda i, j, kk, seq=seq: ((i * tm) // seq, 0, j)))
        args.append(arr)
    scratch = [] if nk == 1 else [pltpu.VMEM((tm, tn), F32)]
    outs = pl.pallas_call(
        functools.partial(_mm_kernel, epilogue=epilogue, n_extra=len(extras), nk=nk),
        out_shape=[jax.ShapeDtypeStruct((m, n), dt) for dt in out_dtypes],
        grid=(m // tm, n // tn, nk),
        in_specs=in_specs,
        out_specs=[pl.BlockSpec((tm, tn), lambda i, j, kk: (i, j))] * len(out_dtypes),
        scratch_shapes=scratch,
        compiler_params=_params("parallel", "parallel", "arbitrary"),
        name="matmul_" + epilogue.__name__.strip("_"),
    )(*args)
    return outs


def _ep_plain(acc, extras, outs):
    outs[0][...] = acc.astype(outs[0].dtype)


def _ep_residual_gate(acc, extras, outs):
    resid_ref, gate_ref = extras
    outs[0][...] = resid_ref[...] + gate_ref[0] * acc


def _ep_head_rms(acc, extras, outs, scale):
    gain_ref, = extras
    tn = acc.shape[1]
    for h in range(tn // HEAD_DIM):
        sl = slice(h * HEAD_DIM, (h + 1) * HEAD_DIM)
        blk = acc[:, sl]
        y = blk * lax.rsqrt(jnp.mean(blk * blk, axis=-1, keepdims=True) + NORM_EPS) * gain_ref[:, sl]
        if scale != 1.0:
            y = y * scale
        outs[0][:, sl] = y.astype(outs[0].dtype)


def _ep_q_norm(acc, extras, outs):
    _ep_head_rms(acc, extras, outs, HEAD_DIM ** -0.5)


def _ep_k_norm(acc, extras, outs):
    _ep_head_rms(acc, extras, outs, 1.0)


def _lora_kernel(x_ref, w1_ref, w2_ref, bias_ref, o_ref, *, act, post):
    t = _dot(x_ref[...], w1_ref[...])
    if act == "tanh":
        t = jnp.tanh(t)
    elif act == "sigmoid":
        t = jax.nn.sigmoid(t)
    z = bias_ref[...] + _dot(t, w2_ref[...])
    if post == "sigmoid":
        z = jax.nn.sigmoid(z)
    elif post == "log_decay":
        z = -math.exp(-0.5) * jax.nn.sigmoid(z)
    o_ref[...] = z.astype(o_ref.dtype)


def lora(x, w1, w2, bias, *, act, post, out_dtype=F32, tm=512):
    m, k = x.shape
    r = w1.shape[1]
    n = w2.shape[1]
    tm = min(tm, m)
    return pl.pallas_call(
        functools.partial(_lora_kernel, act=act, post=post),
        out_shape=jax.ShapeDtypeStruct((m, n), out_dtype),
        grid=(m // tm,),
        in_specs=[pl.BlockSpec((tm, k), lambda i: (i, 0)),
                  pl.BlockSpec((k, r), lambda i: (0, 0)),
                  pl.BlockSpec((r, n), lambda i: (0, 0)),
                  pl.BlockSpec((1, n), lambda i: (0, 0))],
        out_specs=pl.BlockSpec((tm, n), lambda i: (i, 0)),
        compiler_params=_params("parallel"),
        name="lora_" + act + "_" + post,
    )(x, w1, w2, bias)


def _wkv_kernel(*refs, n_pairs, has_vmix):
    if has_vmix:
        (r_ref, k_ref, v_ref, lw_ref, a_ref, g_ref, vf_ref, vg_ref,
         kk_ref, ka_ref, rk_ref, lnw_ref, lnb_ref, o_ref, state_ref) = refs
    else:
        (r_ref, k_ref, v_ref, lw_ref, a_ref, g_ref,
         kk_ref, ka_ref, rk_ref, lnw_ref, lnb_ref, o_ref, state_ref) = refs
    C = WKV_CHUNK
    P2 = 2 * C
    c_idx = pl.program_id(2)

    @pl.when(c_idx == 0)
    def _():
        state_ref[...] = jnp.zeros_like(state_ref)

    lane = lax.broadcasted_iota(jnp.int32, (C, LANES), 1)
    lo_half = lane < RWKV_HEAD
    ri = lax.broadcasted_iota(jnp.int32, (P2, P2), 0)
    ci = lax.broadcasted_iota(jnp.int32, (P2, P2), 1)
    same_head = (ri < C) == (ci < C)
    strict = jnp.logical_and(same_head, ci < ri)
    incl = jnp.logical_and(same_head, ci <= ri)
    block_ones = jnp.where(same_head, 1.0, 0.0).astype(BF16)
    tri_r = lax.broadcasted_iota(jnp.int32, (C, C), 0)
    tri_c = lax.broadcasted_iota(jnp.int32, (C, C), 1)
    tril_ones = jnp.where(tri_c <= tri_r, 1.0, 0.0).astype(BF16)
    ones_sq = jnp.ones((P2, P2), BF16)

    def stack(x):
        return jnp.concatenate([jnp.where(lo_half, x, 0.0), jnp.where(lo_half, 0.0, x)], axis=0)

    def head_sum(x):
        return jnp.dot(x.astype(BF16), block_ones, preferred_element_type=F32)

    pairs = range(n_pairs)
    cols = [slice(p * LANES, (p + 1) * LANES) for p in pairs]

    def each(fn, *lists):
        return [fn(*args) for args in zip(*lists)]

    r = [r_ref[0, :, cs] for cs in cols]
    k = [k_ref[0, :, cs] for cs in cols]
    v = [v_ref[0, :, cs] for cs in cols]
    lw = [lw_ref[0, :, cs] for cs in cols]
    a = [a_ref[0, :, cs] for cs in cols]
    if has_vmix:
        v = [vi + (vf_ref[0, :, cs] - vi) * vg_ref[0, :, cs] for vi, cs in zip(v, cols)]
    kkr = [ki * kk_ref[:, cs] for ki, cs in zip(k, cols)]
    nrm = each(lambda x: jnp.sqrt(head_sum(x * x)), kkr)
    kk = each(lambda x, n: x / jnp.maximum(n, 1e-12), kkr, nrm)
    k2 = [ki * (1.0 + (ai - 1.0) * ka_ref[:, cs]) for ki, ai, cs in zip(k, a, cols)]

    lw_split = each(_split_hi_lo, lw)
    cum = each(lambda hl: jnp.dot(tril_ones, hl[0], preferred_element_type=F32)
               + jnp.dot(tril_ones, hl[1], preferred_element_type=F32), lw_split)
    cum_end_col = each(lambda hl: lax.dot_general(jnp.concatenate(hl, axis=0), ones_sq,
                                                  (((0,), (0,)), ((), ())),
                                                  preferred_element_type=F32), lw_split)

    e_prev = each(lambda c, l: jnp.exp(c - l), cum, lw)
    e_cum = each(jnp.exp, cum)
    e_neg = each(lambda c: jnp.exp(-c), cum)
    e_tail = each(lambda c: jnp.exp(c[C - 1:C, :] - c), cum)
    b = each(lambda x, y: x * y, kk, a)
    a_t = each(lambda x, e: stack(-x * e), kk, e_prev)
    r_t = each(lambda x, e: stack(x * e), r, e_cum)
    b_t = each(lambda x, e: stack(x * e), b, e_neg)
    k_t = each(lambda x, e: stack(x * e), k2, e_neg)
    b_g = each(lambda x, e: stack(x * e), b, e_tail)
    k_g = each(lambda x, e: stack(x * e), k2, e_tail)
    v_s = each(stack, v)

    gram = each(lambda at, rt, bt, kt: _dot_nt(jnp.concatenate([at, rt], axis=0),
                                               jnp.concatenate([bt, kt], axis=0)), a_t, r_t, b_t, k_t)
    p_ab = each(lambda gm: jnp.where(strict, gm[:P2, :P2], 0.0), gram)
    p_ak = each(lambda gm: jnp.where(strict, gm[:P2, P2:], 0.0), gram)
    p_r = each(lambda gm: jnp.concatenate([jnp.where(incl, gm[P2:, :P2], 0.0),
                                           jnp.where(incl, gm[P2:, P2:], 0.0)], axis=1), gram)

    eye = jnp.where(ri == ci, 1.0, 0.0)
    qk = each(lambda pb: _dot(pb, pb), p_ab)
    tk = each(lambda pb: eye + pb, p_ab)
    n_fac = int(math.log2(C))
    for i in range(1, n_fac):
        if i + 1 < n_fac:
            both = each(lambda qi, ti: _dot(qi, jnp.concatenate([qi, ti], axis=1)), qk, tk)
            qk = each(lambda bo: bo[:, :P2], both)
            tk = each(lambda ti, bo: ti + bo[:, P2:], tk, both)
        else:
            tk = each(lambda ti, qi: ti + _dot(qi, ti), tk, qk)
    x = each(lambda ti, at, pk, vs: _dot(ti, jnp.concatenate([at, _dot(pk, vs)], axis=1)),
             tk, a_t, p_ak, v_s)

    rhs = each(lambda xi, vs: jnp.concatenate(
        [xi, jnp.concatenate([jnp.zeros_like(vs), vs], axis=1)], axis=0), x, v_s)
    ry = each(_dot, p_r, rhs)
    gh = each(lambda bg, kg, rh: _dot_tn(jnp.concatenate([bg, kg], axis=0), rh), b_g, k_g, rhs)

    m0 = [state_ref[p] for p in pairs]
    sm = each(lambda rt, ryi, ghi, m: _dot(jnp.concatenate([rt + ryi[:, :P2], ghi[:, :P2]], axis=0), m),
              r_t, ry, gh, m0)
    for p in pairs:
        state_ref[p] = jnp.exp(cum_end_col[p]) * m0[p] + sm[p][P2:] + gh[p][:, P2:]
    y_s = each(lambda s, ryi: s[:P2] + ryi[:, P2:], sm, ry)
    y = each(lambda ys: ys[:C] + ys[C:], y_s)

    mean = each(lambda yi: head_sum(yi) * (1.0 / RWKV_HEAD), y)
    d = each(lambda yi, mi: yi - mi, y, mean)
    var = each(lambda di: head_sum(di * di) * (1.0 / RWKV_HEAD), d)
    bonus_w = [head_sum(ri * ki * rk_ref[:, cs]) for ri, ki, cs in zip(r, k2, cols)]
    for p, cs in zip(pairs, cols):
        yn = d[p] * lax.rsqrt(var[p] + GN_EPS) * lnw_ref[:, cs] + lnb_ref[:, cs]
        o_ref[0, :, cs] = ((yn + bonus_w[p] * v[p]) * g_ref[0, :, cs]).astype(o_ref.dtype)


def wkv7(r, k, v, lw, a, g, v_first, v_gate, k_k, k_a, r_k, ln_w, ln_b, *, width=512):
    bsz, t, c = r.shape
    has_vmix = v_first is not None
    n_pairs = width // LANES
    seq_spec = pl.BlockSpec((1, WKV_CHUNK, width), lambda bi, hg, ci: (bi, ci, hg))
    ch_spec = pl.BlockSpec((1, width), lambda bi, hg, ci: (0, hg))
    seq_args = [r, k, v, lw, a, g] + ([v_first, v_gate] if has_vmix else [])
    ch_args = [k_k, k_a, r_k, ln_w, ln_b]
    return pl.pallas_call(
        functools.partial(_wkv_kernel, n_pairs=n_pairs, has_vmix=has_vmix),
        out_shape=jax.ShapeDtypeStruct((bsz, t, c), BF16),
        grid=(bsz, c // width, t // WKV_CHUNK),
        in_specs=[seq_spec] * len(seq_args) + [ch_spec] * len(ch_args),
        out_specs=seq_spec,
        scratch_shapes=[pltpu.VMEM((n_pairs, LANES, LANES), F32)],
        compiler_params=_params("parallel", "parallel", "arbitrary"),
        name="wkv7",
    )(*seq_args, *ch_args)


def _ffn_in_kernel(x_ref, wg_ref, wu_ref, cwg_ref, cwu_ref, cbg_ref, cbu_ref, o_ref,
                   w_bf, carry_g, carry_u, *, tiles_per_seq):
    i = pl.program_id(1)
    tn = wg_ref.shape[1]

    @pl.when(i == 0)
    def _():
        w_bf[:, :tn] = wg_ref[...].astype(BF16)
        w_bf[:, tn:] = wu_ref[...].astype(BF16)

    @pl.when(i % tiles_per_seq == 0)
    def _():
        carry_g[...] = jnp.zeros_like(carry_g)
        carry_u[...] = jnp.zeros_like(carry_u)

    x = x_ref[...]
    tm = x.shape[0]

    def conv(u, carry_ref, cw_ref, cb_ref):
        row = lax.broadcasted_iota(jnp.int32, u.shape, 0)
        last1 = carry_ref[SUBLANES - 1:SUBLANES, :]
        last2 = carry_ref[SUBLANES - 2:SUBLANES - 1, :]
        u1 = jnp.where(row == 0, last1, pltpu.roll(u, 1, axis=0))
        u2 = jnp.where(row == 0, last2, jnp.where(row == 1, last1, pltpu.roll(u, 2, axis=0)))
        carry_ref[...] = u[tm - SUBLANES:, :]
        return cw_ref[0:1, :] * u + cw_ref[1:2, :] * u1 + cw_ref[2:3, :] * u2 + cb_ref[...]

    u = jnp.dot(x, w_bf[...], preferred_element_type=F32)
    gate = conv(u[:, :tn], carry_g, cwg_ref, cbg_ref)
    up = conv(u[:, tn:], carry_u, cwu_ref, cbu_ref)
    o_ref[...] = (gate * jax.nn.sigmoid(gate) * up).astype(o_ref.dtype)


def ffn_in(h, w_in, conv_w, conv_b, layer, *, seq, tm=1024, tn=256):
    m, k = h.shape
    f = w_in.shape[2] // 2
    nf = f // tn
    return pl.pallas_call(
        functools.partial(_ffn_in_kernel, tiles_per_seq=seq // tm),
        out_shape=jax.ShapeDtypeStruct((m, f), BF16),
        grid=(nf, m // tm),
        in_specs=[pl.BlockSpec((tm, k), lambda j, i: (i, 0)),
                  pl.BlockSpec((None, k, tn), lambda j, i: (layer, 0, j)),
                  pl.BlockSpec((None, k, tn), lambda j, i: (layer, 0, j + nf)),
                  pl.BlockSpec((None, CONV_WIDTH, tn), lambda j, i: (layer, 0, j)),
                  pl.BlockSpec((None, CONV_WIDTH, tn), lambda j, i: (layer, 0, j + nf)),
                  pl.BlockSpec((1, tn), lambda j, i: (0, j)),
                  pl.BlockSpec((1, tn), lambda j, i: (0, j + nf))],
        out_specs=pl.BlockSpec((tm, tn), lambda j, i: (i, j)),
        scratch_shapes=[pltpu.VMEM((k, 2 * tn), BF16),
                        pltpu.VMEM((SUBLANES, tn), F32), pltpu.VMEM((SUBLANES, tn), F32)],
        compiler_params=_params("parallel", "arbitrary"),
        name="ffn_in",
    )(h, w_in, w_in, conv_w, conv_w, conv_b, conv_b)


def _attn_kernel(*refs, n_groups, heads_per_block, seq, jobs_per_call):
    q_refs = refs[0:n_groups]
    k_refs = refs[n_groups:2 * n_groups]
    v_refs = refs[2 * n_groups:3 * n_groups]
    slope_ref, o_ref, qf, kf, vf, m_s, l_s, acc_s = refs[3 * n_groups:]
    blk = ATTN_BLOCK
    qi = lax.broadcasted_iota(jnp.int32, (blk, 2 * blk), 0)
    kj = lax.broadcasted_iota(jnp.int32, (blk, 2 * blk), 1) - blk
    dist = qi - kj

    for g in range(n_groups):
        window, dil = DILATION_PAIRS[g]
        span = window // dil
        nb = seq // (dil * blk)
        in_window = jnp.logical_and(dist >= 0, dist <= span)
        if nb == 1:
            in_window = jnp.logical_and(in_window, kj >= 0)
        bias = [jnp.where(in_window,
                          slope_ref[0:1, h * HEAD_DIM:h * HEAD_DIM + 1] * (-dil * dist).astype(F32), NEG_BIG)
                for h in range(heads_per_block)]
        for h in range(heads_per_block):
            hs = slice(h * HEAD_DIM, (h + 1) * HEAD_DIM)
            qf[h] = q_refs[g][0, :, hs].astype(F32)
            kf[h] = k_refs[g][0, :, hs].astype(F32)
            vf[h] = v_refs[g][0, :, hs].astype(F32)

        def process(jobs, g=g, dil=dil, nb=nb, bias=bias):
            step = dil * blk
            chains = []
            for n, r in jobs:
                if isinstance(n, int):
                    base, base_prev = n * step, max(n - 1, 0) * step
                else:
                    base = pl.multiple_of(n * step, step)
                    base_prev = pl.multiple_of(jnp.maximum(n - 1, 0) * step, step)
                cur = pl.ds(base + r, blk, stride=dil)
                prev = pl.ds(base_prev + r, blk, stride=dil)
                no_prev = None if nb == 1 else jnp.logical_and(kj < 0, n == 0)
                chains += [(h, cur, prev, no_prev) for h in range(heads_per_block)]
            s = [_dot_nt(qf[h, cur, :], jnp.concatenate([kf[h, prev, :], kf[h, cur, :]], axis=0))
                 for h, cur, prev, _ in chains]
            s = [si + bias[h] for si, (h, _, _, _) in zip(s, chains)]
            if nb > 1:
                s = [jnp.where(no_prev, NEG_BIG, si) for si, (_, _, _, no_prev) in zip(s, chains)]
            m_blk = [jnp.max(si, axis=-1, keepdims=True) for si in s]
            p = [jnp.exp(si - mi) for si, mi in zip(s, m_blk)]
            l_blk = [jnp.broadcast_to(jnp.sum(pi, axis=-1, keepdims=True), (blk, HEAD_DIM)) for pi in p]
            o_blk = [_dot(pi, jnp.concatenate([vf[h, prev, :], vf[h, cur, :]], axis=0))
                     for pi, (h, cur, prev, _) in zip(p, chains)]
            m_blk = [jnp.broadcast_to(mi, (blk, HEAD_DIM)) for mi in m_blk]
            for (h, cur, _, _), mi, li, oi in zip(chains, m_blk, l_blk, o_blk):
                if g == 0:
                    m_s[h, cur, :] = mi
                    l_s[h, cur, :] = li
                    acc_s[h, cur, :] = oi
                else:
                    m_old = m_s[h, cur, :]
                    m_new = jnp.maximum(m_old, mi)
                    a_old = jnp.exp(m_old - m_new)
                    a_blk = jnp.exp(mi - m_new)
                    m_s[h, cur, :] = m_new
                    l_s[h, cur, :] = a_old * l_s[h, cur, :] + a_blk * li
                    acc_s[h, cur, :] = a_old * acc_s[h, cur, :] + a_blk * oi

        if nb == 1:
            for r0 in range(0, dil, jobs_per_call):
                process([(0, r) for r in range(r0, min(r0 + jobs_per_call, dil))])
        elif dil >= jobs_per_call:
            for r0 in range(0, dil, jobs_per_call):
                def body(n, carry, r0=r0):
                    process([(n, r) for r in range(r0, r0 + jobs_per_call)])
                    return carry
                lax.fori_loop(0, nb, body, 0)
        else:
            for r in range(dil):
                def body(n2, carry, r=r):
                    process([(n2 * jobs_per_call + j, r) for j in range(jobs_per_call)])
                    return carry
                lax.fori_loop(0, nb // jobs_per_call, body, 0)

    for h in range(heads_per_block):
        o_ref[0, :, h * HEAD_DIM:(h + 1) * HEAD_DIM] = (acc_s[h] / l_s[h]).astype(o_ref.dtype)


def dilated_attention(q, k, v, n_groups, n_heads, heads_per_block=2, jobs_per_call=4):
    bsz, s, qw = q.shape
    gw = n_heads * HEAD_DIM
    w = heads_per_block * HEAD_DIM
    n_hb = n_heads // heads_per_block
    for window, dil in DILATION_PAIRS[:n_groups]:
        assert window // dil <= ATTN_BLOCK and s % (dil * ATTN_BLOCK) == 0
    slopes = jnp.exp2(-8.0 * jnp.arange(1, n_heads + 1, dtype=F32) / n_heads)
    slope_row = jnp.repeat(slopes, HEAD_DIM).reshape(1, gw)
    specs = [pl.BlockSpec((1, s, w), lambda b, hb, g=g: (b, 0, g * n_hb + hb)) for g in range(n_groups)]
    scratch = pltpu.VMEM((heads_per_block, s, HEAD_DIM), F32)
    return pl.pallas_call(
        functools.partial(_attn_kernel, n_groups=n_groups, heads_per_block=heads_per_block, seq=s,
                          jobs_per_call=jobs_per_call),
        out_shape=jax.ShapeDtypeStruct((bsz, s, gw), BF16),
        grid=(bsz, n_hb),
        in_specs=specs * 3 + [pl.BlockSpec((1, w), lambda b, hb: (0, hb))],
        out_specs=pl.BlockSpec((1, s, w), lambda b, hb: (b, 0, hb)),
        scratch_shapes=[scratch] * 6,
        compiler_params=_params("parallel", "parallel"),
        name="dilated_attn",
    )(*([q] * n_groups + [k] * n_groups + [v] * n_groups), slope_row)


PROJ_TM = 2048
PROJ_TN = 256
RESID_TM = 1024
RESID_TN = 512


def _row(vec):
    return vec.reshape(1, -1)


def _tile(pref, dim):
    return min(pref, dim)


def rwkv_layer(x, a_mix, sh_mix, gate_mix, mu, w_r, w_k, w_v, w_o, idx, w0, w1, w2, a0, a1, a2, g1, g2,
               k_k, k_a, r_k, ln_w, ln_b, v_first, v_mix):
    bsz, seq, d = x.shape
    m = bsz * seq
    tm, tn = _tile(PROJ_TM, seq), _tile(PROJ_TN, d)
    bf = lambda w: w.astype(BF16)
    xr, xw, xk, xv, xa, xg = (t.reshape(m, d) for t in norm_token_mix(x, a_mix, sh_mix, mu))
    r, = matmul(xr, w_r, _ep_plain, [F32], n=d, tm=tm, tn=tn, layer=idx)
    k, = matmul(xk, w_k, _ep_plain, [F32], n=d, tm=tm, tn=tn, layer=idx)
    v, = matmul(xv, w_v, _ep_plain, [F32], n=d, tm=tm, tn=tn, layer=idx)
    lw = lora(xw, bf(w1), bf(w2), _row(w0), act="tanh", post="log_decay")
    a = lora(xa, bf(a1), bf(a2), _row(a0), act="none", post="sigmoid")
    g = lora(xg, bf(g1), bf(g2), jnp.zeros((1, d), F32), act="sigmoid", post="none")
    if v_mix is None:
        v_first = v
        vf = v_gate = None
    else:
        v0, v1, v2 = v_mix
        v_gate = lora(xv, bf(v1), bf(v2), _row(v0), act="none", post="sigmoid")
        vf = v_first
    to3 = lambda t: None if t is None else t.reshape(bsz, seq, d)
    y = wkv7(to3(r), to3(k), to3(v), to3(lw), to3(a), to3(g), to3(vf), to3(v_gate),
             _row(k_k), _row(k_a), _row(r_k), _row(ln_w), _row(ln_b), width=_tile(2048, d))
    x2, = matmul(y.reshape(m, d), w_o, _ep_residual_gate, [F32], n=d, tm=_tile(RESID_TM, seq),
                 tn=_tile(RESID_TN, d), layer=idx,
                 extras=[(x.reshape(m, d), "tile"), (gate_mix, "batch")])
    return x2.reshape(bsz, seq, d), v_first


def attn_layer(x, a_mix, sh_mix, gate_mix, w_q, w_o, idx, q_gain, kv_params, k_shared, v_shared):
    bsz, seq, d = x.shape
    m = bsz * seq
    q_width = w_q.shape[2]
    n_groups = q_gain.shape[0]
    group_w = q_width // n_groups
    n_heads = group_w // HEAD_DIM
    tm, tn = _tile(PROJ_TM, seq), _tile(PROJ_TN, group_w)
    bf = lambda w: w.astype(BF16)
    if kv_params is not None:
        kv_norm, w_kv, k_gain = kv_params
        a_all = jnp.concatenate([a_mix, jnp.broadcast_to(kv_norm[None, None, :], (bsz, 1, d))], axis=1)
        b_all = jnp.concatenate([sh_mix, jnp.zeros((bsz, 1, d), F32)], axis=1)
        h, hkv = norm_affine(x, a_all, b_all)
        k_gain_row = jnp.tile(k_gain, (1, n_heads)).reshape(1, q_width)
        k_shared, = matmul(hkv.reshape(m, d), w_kv, _ep_k_norm, [BF16], n=q_width, tm=tm, tn=tn,
                           extras=[(k_gain_row, "col")])
        v_shared, = matmul(hkv.reshape(m, d), w_kv, _ep_plain, [BF16], n=q_width, col_off=q_width,
                           tm=tm, tn=tn)
        k_shared = k_shared.reshape(bsz, seq, q_width)
        v_shared = v_shared.reshape(bsz, seq, q_width)
    else:
        h, = norm_affine(x, a_mix, sh_mix)
    q_gain_row = jnp.tile(q_gain, (1, n_heads)).reshape(1, q_width)
    q, = matmul(h.reshape(m, d), w_q, _ep_q_norm, [BF16], n=q_width, tm=tm, tn=tn, layer=idx,
                extras=[(q_gain_row, "col")])
    q = q.reshape(bsz, seq, q_width)
    merged = dilated_attention(q, k_shared, v_shared, n_groups, n_heads).reshape(m, group_w)
    x2, = matmul(merged, w_o, _ep_residual_gate, [F32], n=d, tm=_tile(RESID_TM, seq),
                 tn=_tile(RESID_TN, d), layer=idx,
                 extras=[(x.reshape(m, d), "tile"), (gate_mix, "batch")])
    return x2.reshape(bsz, seq, d), k_shared, v_shared


def ffn_layer(x, a_ffn, sh_ffn, gate_ffn, w_in_all, conv_w_all, layer, conv_b, w_out_all, ffn_tn=256, nk=1):
    bsz, seq, d = x.shape
    m = bsz * seq
    h, = norm_affine(x, a_ffn, sh_ffn)
    act = ffn_in(h.reshape(m, d), w_in_all, conv_w_all, _row(conv_b), layer, seq=seq,
                 tm=_tile(1024, seq), tn=ffn_tn)
    x2, = matmul(act, w_out_all, _ep_residual_gate, [F32], n=d, tm=_tile(512, seq), tn=_tile(512, d),
                 nk=nk, layer=layer, extras=[(x.reshape(m, d), "tile"), (gate_ffn, "batch")])
    return x2.reshape(bsz, seq, d)


def kernel(x, c, ada_w, ada_table, norm_mix, norm_ffn, rwkv_mu, rwkv_w_r, rwkv_w_k, rwkv_w_v, rwkv_w_o, rwkv_w0, rwkv_w1, rwkv_w2, rwkv_a0, rwkv_a1, rwkv_a2, rwkv_g1, rwkv_g2, rwkv_k_k, rwkv_k_a, rwkv_r_k, rwkv_ln_w, rwkv_ln_b, rwkv_v0, rwkv_v1, rwkv_v2, kv_norm, attn_w_kv, attn_k_gain, attn_w_q, attn_q_gain, attn_w_o, ffn_w_in, ffn_conv_w, ffn_conv_b, ffn_w_out):
    bsz, seq, d = x.shape
    depth = ada_table.shape[0]
    n_a = rwkv_w_r.shape[0]
    cond = ada_cond(c, ada_w).reshape(bsz, 6, d)
    ffn_w_out_bf = ffn_w_out.astype(BF16)
    v_first = None
    k_shared = v_shared = None
    for layer in range(depth):
        mod = cond + ada_table[layer][None]
        sh_mix, sc_mix, gate_mix, sh_ffn, sc_ffn, gate_ffn = (mod[:, i:i + 1, :] for i in range(6))
        a_mix = norm_mix[layer][None, None, :] * (1.0 + sc_mix)
        if layer < n_a:
            i = layer
            v_mix = None if i == 0 else (rwkv_v0[i - 1], rwkv_v1[i - 1], rwkv_v2[i - 1])
            x, v_first = rwkv_layer(
                x, a_mix, sh_mix, gate_mix, rwkv_mu[i], rwkv_w_r, rwkv_w_k, rwkv_w_v, rwkv_w_o, i,
                rwkv_w0[i], rwkv_w1[i], rwkv_w2[i], rwkv_a0[i], rwkv_a1[i], rwkv_a2[i], rwkv_g1[i], rwkv_g2[i],
                rwkv_k_k[i], rwkv_k_a[i], rwkv_r_k[i], rwkv_ln_w[i], rwkv_ln_b[i], v_first, v_mix)
        else:
            j = layer - n_a
            kv_params = (kv_norm, attn_w_kv, attn_k_gain) if j == 0 else None
            x, k_shared, v_shared = attn_layer(x, a_mix, sh_mix, gate_mix, attn_w_q, attn_w_o, j,
                                               attn_q_gain[j], kv_params, k_shared, v_shared)
        a_ffn = norm_ffn[layer][None, None, :] * (1.0 + sc_ffn)
        x = ffn_layer(x, a_ffn, sh_ffn, gate_ffn, ffn_w_in, ffn_conv_w, layer, ffn_conv_b[layer],
                      ffn_w_out_bf)
    return x
```

```python
import functools
import math

import jax
import jax.numpy as jnp
from jax import lax
from jax.experimental import pallas as pl
from jax.experimental.pallas import tpu as pltpu

F32 = jnp.float32
BF16 = jnp.bfloat16

VMEM_LIMIT_BYTES = 56 * 1024 * 1024
LANES = 128
SUBLANES = 8

NORM_EPS = 1e-6
GN_EPS = 64e-5
RWKV_HEAD = 64
WKV_CHUNK = 64
HEAD_DIM = 128
ATTN_BLOCK = 128
DILATION_PAIRS = ((128, 1), (512, 4), (2048, 16))
CONV_WIDTH = 3
NEG_BIG = -1e30


def _params(*sem):
    return pltpu.CompilerParams(dimension_semantics=sem, vmem_limit_bytes=VMEM_LIMIT_BYTES)


def _dot(a, b):
    return jnp.dot(a.astype(BF16), b.astype(BF16), preferred_element_type=F32)


def _dot_nt(a, b):
    return lax.dot_general(a.astype(BF16), b.astype(BF16), (((1,), (1,)), ((), ())),
                           preferred_element_type=F32)


def _dot_tn(a, b):
    return lax.dot_general(a.astype(BF16), b.astype(BF16), (((0,), (0,)), ((), ())),
                           preferred_element_type=F32)


def _split_hi_lo(x):
    hi = x.astype(BF16)
    lo = (x - hi.astype(F32)).astype(BF16)
    return hi, lo


def _ada_kernel(c_ref, w_ref, o_ref):
    c = c_ref[...]
    o_ref[...] = _dot(c * jax.nn.sigmoid(c), w_ref[...])


def ada_cond(c, ada_w, tn=512):
    b, d = c.shape
    n = ada_w.shape[1]
    cp = jnp.pad(c, ((0, SUBLANES - b), (0, 0)))
    out = pl.pallas_call(
        _ada_kernel,
        out_shape=jax.ShapeDtypeStruct((SUBLANES, n), F32),
        grid=(n // tn,),
        in_specs=[pl.BlockSpec((SUBLANES, d), lambda j: (0, 0)),
                  pl.BlockSpec((d, tn), lambda j: (0, j))],
        out_specs=pl.BlockSpec((SUBLANES, tn), lambda j: (0, j)),
        compiler_params=_params("parallel"),
        name="ada_cond",
    )(cp, ada_w)
    return out[:b]


NORM_ROWS = 16
NORM_COLS = 512


def _inv_rms(load_block, d):
    ss = None
    for j in range(d // NORM_COLS):
        xb = load_block(slice(j * NORM_COLS, (j + 1) * NORM_COLS))
        sq = xb * xb
        ss = sq if ss is None else ss + sq
    return lax.rsqrt(jnp.sum(ss, axis=-1, keepdims=True) * (1.0 / d) + NORM_EPS)


def _norm_affine_kernel(x_ref, a_ref, b_ref, *o_refs):
    x = x_ref[0]
    xn = x * lax.rsqrt(jnp.mean(x * x, axis=-1, keepdims=True) + NORM_EPS)
    for k, o_ref in enumerate(o_refs):
        o_ref[0] = (xn * a_ref[0, k:k + 1, :] + b_ref[0, k:k + 1, :]).astype(o_ref.dtype)


def norm_affine(x, a, b, tm=256):
    bsz, s, d = x.shape
    n_out = a.shape[1]
    outs = pl.pallas_call(
        _norm_affine_kernel,
        out_shape=[jax.ShapeDtypeStruct((bsz, s, d), BF16)] * n_out,
        grid=(bsz, s // tm),
        in_specs=[pl.BlockSpec((1, tm, d), lambda bi, i: (bi, i, 0)),
                  pl.BlockSpec((1, n_out, d), lambda bi, i: (bi, 0, 0)),
                  pl.BlockSpec((1, n_out, d), lambda bi, i: (bi, 0, 0))],
        out_specs=[pl.BlockSpec((1, tm, d), lambda bi, i: (bi, i, 0))] * n_out,
        compiler_params=_params("parallel", "parallel"),
        name="norm_affine",
    )(x, a, b)
    return outs


def _norm_mix_kernel(x_ref, xp_ref, a_ref, b_ref, mu_ref, *o_refs):
    i = pl.program_id(1)
    tm, d = x_ref.shape[1], x_ref.shape[2]
    row = lax.broadcasted_iota(jnp.int32, (NORM_ROWS, NORM_COLS), 0)

    def chunk(c, carry):
        r0 = pl.multiple_of(c * NORM_ROWS, NORM_ROWS)
        rows = pl.ds(r0, NORM_ROWS)
        before = pl.ds(pl.multiple_of(jnp.maximum(r0 - SUBLANES, 0), SUBLANES), SUBLANES)

        def shifted(cols):
            prev8 = jnp.where(c > 0, x_ref[0, before, cols], xp_ref[0, :, cols])
            return jnp.where(row == 0, prev8[SUBLANES - 1:SUBLANES, :],
                             pltpu.roll(x_ref[0, rows, cols], 1, axis=0))

        inv = _inv_rms(lambda cols: x_ref[0, rows, cols], d)
        inv_s = _inv_rms(shifted, d)
        seq_first = jnp.logical_and(jnp.logical_and(c == 0, i == 0), row == 0)
        for j in range(d // NORM_COLS):
            cols = slice(j * NORM_COLS, (j + 1) * NORM_COLS)
            a = a_ref[0, :, cols]
            b = b_ref[0, :, cols]
            h = x_ref[0, rows, cols] * inv * a + b
            h_prev = jnp.where(seq_first, 0.0, shifted(cols) * inv_s * a + b)
            xx = h_prev - h
            for k, o_ref in enumerate(o_refs):
                o_ref[0, rows, cols] = (h + xx * mu_ref[k:k + 1, cols]).astype(o_ref.dtype)
        return carry

    lax.fori_loop(0, tm // NORM_ROWS, chunk, 0)


def norm_token_mix(x, a, b, mu, tm=256):
    bsz, s, d = x.shape
    n_out = mu.shape[0]
    rows_per_blk = tm // SUBLANES
    outs = pl.pallas_call(
        _norm_mix_kernel,
        out_shape=[jax.ShapeDtypeStruct((bsz, s, d), BF16)] * n_out,
        grid=(bsz, s // tm),
        in_specs=[pl.BlockSpec((1, tm, d), lambda bi, i: (bi, i, 0)),
                  pl.BlockSpec((1, SUBLANES, d),
                               lambda bi, i: (bi, jnp.maximum(i * rows_per_blk - 1, 0), 0)),
                  pl.BlockSpec((1, 1, d), lambda bi, i: (bi, 0, 0)),
                  pl.BlockSpec((1, 1, d), lambda bi, i: (bi, 0, 0)),
                  pl.BlockSpec((n_out, d), lambda bi, i: (0, 0))],
        out_specs=[pl.BlockSpec((1, tm, d), lambda bi, i: (bi, i, 0))] * n_out,
        compiler_params=_params("parallel", "parallel"),
        name="norm_mix",
    )(x, x, a, b, mu)
    return outs


def _mm_kernel(x_ref, w_ref, *rest, epilogue, n_extra, nk):
    extras = rest[:n_extra]
    if nk == 1:
        outs = rest[n_extra:]
        epilogue(_dot(x_ref[...], w_ref[...]), extras, outs)
        return
    outs = rest[n_extra:-1]
    acc_ref = rest[-1]
    kk = pl.program_id(2)
    part = _dot(x_ref[...], w_ref[...])

    @pl.when(kk == 0)
    def _():
        acc_ref[...] = part

    @pl.when(kk > 0)
    def _():
        acc_ref[...] += part

    @pl.when(kk == nk - 1)
    def _():
        epilogue(acc_ref[...], extras, outs)


def matmul(x, w, epilogue, out_dtypes, *, n, col_off=0, tm, tn, nk=1, extras=(), layer=None):
    m, k = x.shape
    tk = k // nk
    assert m % tm == 0 and n % tn == 0 and col_off % tn == 0 and k % nk == 0
    joff = col_off // tn
    if layer is None:
        w_spec = pl.BlockSpec((tk, tn), lambda i, j, kk: (kk, j + joff))
    else:
        w_spec = pl.BlockSpec((None, tk, tn), lambda i, j, kk: (layer, kk, j + joff))
    in_specs = [pl.BlockSpec((tm, tk), lambda i, j, kk: (i, kk)), w_spec]
    args = [x, w]
    for arr, kind in extras:
        if kind == "tile":
            in_specs.append(pl.BlockSpec((tm, tn), lambda i, j, kk: (i, j)))
        elif kind == "col":
            in_specs.append(pl.BlockSpec((1, tn), lambda i, j, kk: (0, j)))
        else:
            seq = m // arr.shape[0]
            in_specs.append(pl.BlockSpec((1, 1, tn), lambda i, j, kk, seq=seq: ((i * tm) // seq, 0, j)))
        args.append(arr)
    scratch = [] if nk == 1 else [pltpu.VMEM((tm, tn), F32)]
    outs = pl.pallas_call(
        functools.partial(_mm_kernel, epilogue=epilogue, n_extra=len(extras), nk=nk),
        out_shape=[jax.ShapeDtypeStruct((m, n), dt) for dt in out_dtypes],
        grid=(m // tm, n // tn, nk),
        in_specs=in_specs,
        out_specs=[pl.BlockSpec((tm, tn), lambda i, j, kk: (i, j))] * len(out_dtypes),
        scratch_shapes=scratch,
        compiler_params=_params("parallel", "parallel", "arbitrary"),
        name="matmul_" + epilogue.__name__.strip("_"),
    )(*args)
    return outs


def _ep_plain(acc, extras, outs):
    outs[0][...] = acc.astype(outs[0].dtype)


def _ep_residual_gate(acc, extras, outs):
    resid_ref, gate_ref = extras
    outs[0][...] = resid_ref[...] + gate_ref[0] * acc


def _ep_head_rms(acc, extras, outs, scale):
    gain_ref, = extras
    tn = acc.shape[1]
    for h in range(tn // HEAD_DIM):
        sl = slice(h * HEAD_DIM, (h + 1) * HEAD_DIM)
        blk = acc[:, sl]
        y = blk * lax.rsqrt(jnp.mean(blk * blk, axis=-1, keepdims=True) + NORM_EPS) * gain_ref[:, sl]
        if scale != 1.0:
            y = y * scale
        outs[0][:, sl] = y.astype(outs[0].dtype)


def _ep_q_norm(acc, extras, outs):
    _ep_head_rms(acc, extras, outs, HEAD_DIM ** -0.5)


def _ep_k_norm(acc, extras, outs):
    _ep_head_rms(acc, extras, outs, 1.0)


def _lora_kernel(x_ref, w1_ref, w2_ref, bias_ref, o_ref, *, act, post):
    t = _dot(x_ref[...], w1_ref[...])
    if act == "tanh":
        t = jnp.tanh(t)
    elif act == "sigmoid":
        t = jax.nn.sigmoid(t)
    z = bias_ref[...] + _dot(t, w2_ref[...])
    if post == "sigmoid":
        z = jax.nn.sigmoid(z)
    elif post == "log_decay":
        z = -math.exp(-0.5) * jax.nn.sigmoid(z)
    o_ref[...] = z.astype(o_ref.dtype)


def lora(x, w1, w2, bias, *, act, post, out_dtype=F32, tm=512):
    m, k = x.shape
    r = w1.shape[1]
    n = w2.shape[1]
    tm = min(tm, m)
    return pl.pallas_call(
        functools.partial(_lora_kernel, act=act, post=post),
        out_shape=jax.ShapeDtypeStruct((m, n), out_dtype),
        grid=(m // tm,),
        in_specs=[pl.BlockSpec((tm, k), lambda i: (i, 0)),
                  pl.BlockSpec((k, r), lambda i: (0, 0)),
                  pl.BlockSpec((r, n), lambda i: (0, 0)),
                  pl.BlockSpec((1, n), lambda i: (0, 0))],
        out_specs=pl.BlockSpec((tm, n), lambda i: (i, 0)),
        compiler_params=_params("parallel"),
        name="lora_" + act + "_" + post,
    )(x, w1, w2, bias)


def _wkv_kernel(*refs, n_pairs, has_vmix):
    if has_vmix:
        (r_ref, k_ref, v_ref, lw_ref, a_ref, g_ref, vf_ref, vg_ref,
         kk_ref, ka_ref, rk_ref, lnw_ref, lnb_ref, o_ref, state_ref) = refs
    else:
        (r_ref, k_ref, v_ref, lw_ref, a_ref, g_ref,
         kk_ref, ka_ref, rk_ref, lnw_ref, lnb_ref, o_ref, state_ref) = refs
    C = WKV_CHUNK
    P2 = 2 * C
    c_idx = pl.program_id(2)

    @pl.when(c_idx == 0)
    def _():
        state_ref[...] = jnp.zeros_like(state_ref)

    lane = lax.broadcasted_iota(jnp.int32, (C, LANES), 1)
    lo_half = lane < RWKV_HEAD
    ri = lax.broadcasted_iota(jnp.int32, (P2, P2), 0)
    ci = lax.broadcasted_iota(jnp.int32, (P2, P2), 1)
    same_head = (ri < C) == (ci < C)
    strict = jnp.logical_and(same_head, ci < ri)
    incl = jnp.logical_and(same_head, ci <= ri)
    block_ones = jnp.where(same_head, 1.0, 0.0).astype(BF16)
    tri_r = lax.broadcasted_iota(jnp.int32, (C, C), 0)
    tri_c = lax.broadcasted_iota(jnp.int32, (C, C), 1)
    tril_ones = jnp.where(tri_c <= tri_r, 1.0, 0.0).astype(BF16)
    ones_sq = jnp.ones((P2, P2), BF16)

    def stack(x):
        return jnp.concatenate([jnp.where(lo_half, x, 0.0), jnp.where(lo_half, 0.0, x)], axis=0)

    def head_sum(x):
        return jnp.dot(x.astype(BF16), block_ones, preferred_element_type=F32)

    pairs = range(n_pairs)
    cols = [slice(p * LANES, (p + 1) * LANES) for p in pairs]

    def each(fn, *lists):
        return [fn(*args) for args in zip(*lists)]

    r = [r_ref[0, :, cs] for cs in cols]
    k = [k_ref[0, :, cs] for cs in cols]
    v = [v_ref[0, :, cs] for cs in cols]
    lw = [lw_ref[0, :, cs] for cs in cols]
    a = [a_ref[0, :, cs] for cs in cols]
    if has_vmix:
        v = [vi + (vf_ref[0, :, cs] - vi) * vg_ref[0, :, cs] for vi, cs in zip(v, cols)]
    kkr = [ki * kk_ref[:, cs] for ki, cs in zip(k, cols)]
    nrm = each(lambda x: jnp.sqrt(head_sum(x * x)), kkr)
    kk = each(lambda x, n: x / jnp.maximum(n, 1e-12), kkr, nrm)
    k2 = [ki * (1.0 + (ai - 1.0) * ka_ref[:, cs]) for ki, ai, cs in zip(k, a, cols)]

    lw_split = each(_split_hi_lo, lw)
    cum = each(lambda hl: jnp.dot(tril_ones, hl[0], preferred_element_type=F32)
               + jnp.dot(tril_ones, hl[1], preferred_element_type=F32), lw_split)
    cum_end_col = each(lambda hl: lax.dot_general(jnp.concatenate(hl, axis=0), ones_sq,
                                                  (((0,), (0,)), ((), ())),
                                                  preferred_element_type=F32), lw_split)

    e_prev = each(lambda c, l: jnp.exp(c - l), cum, lw)
    e_cum = each(jnp.exp, cum)
    e_neg = each(lambda c: jnp.exp(-c), cum)
    e_tail = each(lambda c: jnp.exp(c[C - 1:C, :] - c), cum)
    b = each(lambda x, y: x * y, kk, a)
    a_t = each(lambda x, e: stack(-x * e), kk, e_prev)
    r_t = each(lambda x, e: stack(x * e), r, e_cum)
    b_t = each(lambda x, e: stack(x * e), b, e_neg)
    k_t = each(lambda x, e: stack(x * e), k2, e_neg)
    b_g = each(lambda x, e: stack(x * e), b, e_tail)
    k_g = each(lambda x, e: stack(x * e), k2, e_tail)
    v_s = each(stack, v)

    gram = each(lambda at, rt, bt, kt: _dot_nt(jnp.concatenate([at, rt], axis=0),
                                               jnp.concatenate([bt, kt], axis=0)), a_t, r_t, b_t, k_t)
    p_ab = each(lambda gm: jnp.where(strict, gm[:P2, :P2], 0.0), gram)
    p_ak = each(lambda gm: jnp.where(strict, gm[:P2, P2:], 0.0), gram)
    p_r = each(lambda gm: jnp.concatenate([jnp.where(incl, gm[P2:, :P2], 0.0),
                                           jnp.where(incl, gm[P2:, P2:], 0.0)], axis=1), gram)

    eye = jnp.where(ri == ci, 1.0, 0.0)
    qk = each(lambda pb: _dot(pb, pb), p_ab)
    tk = each(lambda pb: eye + pb, p_ab)
    n_fac = int(math.log2(C))
    for i in range(1, n_fac):
        if i + 1 < n_fac:
            both = each(lambda qi, ti: _dot(qi, jnp.concatenate([qi, ti], axis=1)), qk, tk)
            qk = each(lambda bo: bo[:, :P2], both)
            tk = each(lambda ti, bo: ti + bo[:, P2:], tk, both)
        else:
            tk = each(lambda ti, qi: ti + _dot(qi, ti), tk, qk)
    x = each(lambda ti, at, pk, vs: _dot(ti, jnp.concatenate([at, _dot(pk, vs)], axis=1)),
             tk, a_t, p_ak, v_s)

    rhs = each(lambda xi, vs: jnp.concatenate(
        [xi, jnp.concatenate([jnp.zeros_like(vs), vs], axis=1)], axis=0), x, v_s)
    ry = each(_dot, p_r, rhs)
    gh = each(lambda bg, kg, rh: _dot_tn(jnp.concatenate([bg, kg], axis=0), rh), b_g, k_g, rhs)

    m0 = [state_ref[p] for p in pairs]
    sm = each(lambda rt, ryi, ghi, m: _dot(jnp.concatenate([rt + ryi[:, :P2], ghi[:, :P2]], axis=0), m),
              r_t, ry, gh, m0)
    for p in pairs:
        state_ref[p] = jnp.exp(cum_end_col[p]) * m0[p] + sm[p][P2:] + gh[p][:, P2:]
    y_s = each(lambda s, ryi: s[:P2] + ryi[:, P2:], sm, ry)
    y = each(lambda ys: ys[:C] + ys[C:], y_s)

    mean = each(lambda yi: head_sum(yi) * (1.0 / RWKV_HEAD), y)
    d = each(lambda yi, mi: yi - mi, y, mean)
    var = each(lambda di: head_sum(di * di) * (1.0 / RWKV_HEAD), d)
    bonus_w = [head_sum(ri * ki * rk_ref[:, cs]) for ri, ki, cs in zip(r, k2, cols)]
    for p, cs in zip(pairs, cols):
        yn = d[p] * lax.rsqrt(var[p] + GN_EPS) * lnw_ref[:, cs] + lnb_ref[:, cs]
        o_ref[0, :, cs] = ((yn + bonus_w[p] * v[p]) * g_ref[0, :, cs]).astype(o_ref.dtype)


def wkv7(r, k, v, lw, a, g, v_first, v_gate, k_k, k_a, r_k, ln_w, ln_b, *, width=512):
    bsz, t, c = r.shape
    has_vmix = v_first is not None
    n_pairs = width // LANES
    seq_spec = pl.BlockSpec((1, WKV_CHUNK, width), lambda bi, hg, ci: (bi, ci, hg))
    ch_spec = pl.BlockSpec((1, width), lambda bi, hg, ci: (0, hg))
    seq_args = [r, k, v, lw, a, g] + ([v_first, v_gate] if has_vmix else [])
    ch_args = [k_k, k_a, r_k, ln_w, ln_b]
    return pl.pallas_call(
        functools.partial(_wkv_kernel, n_pairs=n_pairs, has_vmix=has_vmix),
        out_shape=jax.ShapeDtypeStruct((bsz, t, c), BF16),
        grid=(bsz, c // width, t // WKV_CHUNK),
        in_specs=[seq_spec] * len(seq_args) + [ch_spec] * len(ch_args),
        out_specs=seq_spec,
        scratch_shapes=[pltpu.VMEM((n_pairs, LANES, LANES), F32)],
        compiler_params=_params("parallel", "parallel", "arbitrary"),
        name="wkv7",
    )(*seq_args, *ch_args)


def _ffn_in_kernel(x_ref, wg_ref, wu_ref, cwg_ref, cwu_ref, cbg_ref, cbu_ref, o_ref,
                   w_bf, carry_g, carry_u, *, tiles_per_seq):
    i = pl.program_id(1)
    tn = wg_ref.shape[1]

    @pl.when(i == 0)
    def _():
        w_bf[:, :tn] = wg_ref[...].astype(BF16)
        w_bf[:, tn:] = wu_ref[...].astype(BF16)

    @pl.when(i % tiles_per_seq == 0)
    def _():
        carry_g[...] = jnp.zeros_like(carry_g)
        carry_u[...] = jnp.zeros_like(carry_u)

    x = x_ref[...]
    tm = x.shape[0]

    def conv(u, carry_ref, cw_ref, cb_ref):
        row = lax.broadcasted_iota(jnp.int32, u.shape, 0)
        last1 = carry_ref[SUBLANES - 1:SUBLANES, :]
        last2 = carry_ref[SUBLANES - 2:SUBLANES - 1, :]
        u1 = jnp.where(row == 0, last1, pltpu.roll(u, 1, axis=0))
        u2 = jnp.where(row == 0, last2, jnp.where(row == 1, last1, pltpu.roll(u, 2, axis=0)))
        carry_ref[...] = u[tm - SUBLANES:, :]
        return cw_ref[0:1, :] * u + cw_ref[1:2, :] * u1 + cw_ref[2:3, :] * u2 + cb_ref[...]

    u = jnp.dot(x, w_bf[...], preferred_element_type=F32)
    gate = conv(u[:, :tn], carry_g, cwg_ref, cbg_ref)
    up = conv(u[:, tn:], carry_u, cwu_ref, cbu_ref)
    o_ref[...] = (gate * jax.nn.sigmoid(gate) * up).astype(o_ref.dtype)


def ffn_in(h, w_in, conv_w, conv_b, layer, *, seq, tm=1024, tn=256):
    m, k = h.shape
    f = w_in.shape[2] // 2
    nf = f // tn
    return pl.pallas_call(
        functools.partial(_ffn_in_kernel, tiles_per_seq=seq // tm),
        out_shape=jax.ShapeDtypeStruct((m, f), BF16),
        grid=(nf, m // tm),
        in_specs=[pl.BlockSpec((tm, k), lambda j, i: (i, 0)),
                  pl.BlockSpec((None, k, tn), lambda j, i: (layer, 0, j)),
                  pl.BlockSpec((None, k, tn), lambda j, i: (layer, 0, j + nf)),
                  pl.BlockSpec((None, CONV_WIDTH, tn), lambda j, i: (layer, 0, j)),
                  pl.BlockSpec((None, CONV_WIDTH, tn), lambda j, i: (layer, 0, j + nf)),
                  pl.BlockSpec((1, tn), lambda j, i: (0, j)),
                  pl.BlockSpec((1, tn), lambda j, i: (0, j + nf))],
        out_specs=pl.BlockSpec((tm, tn), lambda j, i: (i, j)),
        scratch_shapes=[pltpu.VMEM((k, 2 * tn), BF16),
                        pltpu.VMEM((SUBLANES, tn), F32), pltpu.VMEM((SUBLANES, tn), F32)],
        compiler_params=_params("parallel", "arbitrary"),
        name="ffn_in",
    )(h, w_in, w_in, conv_w, conv_w, conv_b, conv_b)


def _attn_kernel(*refs, n_groups, heads_per_block, seq, jobs_per_call):
    q_refs = refs[0:n_groups]
    k_refs = refs[n_groups:2 * n_groups]
    v_refs = refs[2 * n_groups:3 * n_groups]
    slope_ref, o_ref, qf, kf, vf, m_s, l_s, acc_s = refs[3 * n_groups:]
    blk = ATTN_BLOCK
    qi = lax.broadcasted_iota(jnp.int32, (blk, 2 * blk), 0)
    kj = lax.broadcasted_iota(jnp.int32, (blk, 2 * blk), 1) - blk
    dist = qi - kj

    for g in range(n_groups):
        window, dil = DILATION_PAIRS[g]
        span = window // dil
        nb = seq // (dil * blk)
        in_window = jnp.logical_and(dist >= 0, dist <= span)
        if nb == 1:
            in_window = jnp.logical_and(in_window, kj >= 0)
        bias = [jnp.where(in_window,
                          slope_ref[0:1, h * HEAD_DIM:h * HEAD_DIM + 1] * (-dil * dist).astype(F32), NEG_BIG)
                for h in range(heads_per_block)]
        for h in range(heads_per_block):
            hs = slice(h * HEAD_DIM, (h + 1) * HEAD_DIM)
            qf[h] = q_refs[g][0, :, hs].astype(F32)
            kf[h] = k_refs[g][0, :, hs].astype(F32)
            vf[h] = v_refs[g][0, :, hs].astype(F32)

        def process(jobs, g=g, dil=dil, nb=nb, bias=bias):
            step = dil * blk
            chains = []
            for n, r in jobs:
                if isinstance(n, int):
                    base, base_prev = n * step, max(n - 1, 0) * step
                else:
                    base = pl.multiple_of(n * step, step)
                    base_prev = pl.multiple_of(jnp.maximum(n - 1, 0) * step, step)
                cur = pl.ds(base + r, blk, stride=dil)
                prev = pl.ds(base_prev + r, blk, stride=dil)
                no_prev = None if nb == 1 else jnp.logical_and(kj < 0, n == 0)
                chains += [(h, cur, prev, no_prev) for h in range(heads_per_block)]
            s = [_dot_nt(qf[h, cur, :], jnp.concatenate([kf[h, prev, :], kf[h, cur, :]], axis=0))
                 for h, cur, prev, _ in chains]
            s = [si + bias[h] for si, (h, _, _, _) in zip(s, chains)]
            if nb > 1:
                s = [jnp.where(no_prev, NEG_BIG, si) for si, (_, _, _, no_prev) in zip(s, chains)]
            m_blk = [jnp.max(si, axis=-1, keepdims=True) for si in s]
            p = [jnp.exp(si - mi) for si, mi in zip(s, m_blk)]
            l_blk = [jnp.broadcast_to(jnp.sum(pi, axis=-1, keepdims=True), (blk, HEAD_DIM)) for pi in p]
            o_blk = [_dot(pi, jnp.concatenate([vf[h, prev, :], vf[h, cur, :]], axis=0))
                     for pi, (h, cur, prev, _) in zip(p, chains)]
            m_blk = [jnp.broadcast_to(mi, (blk, HEAD_DIM)) for mi in m_blk]
            for (h, cur, _, _), mi, li, oi in zip(chains, m_blk, l_blk, o_blk):
                if g == 0:
                    m_s[h, cur, :] = mi
                    l_s[h, cur, :] = li
                    acc_s[h, cur, :] = oi
                else:
                    m_old = m_s[h, cur, :]
                    m_new = jnp.maximum(m_old, mi)
                    a_old = jnp.exp(m_old - m_new)
                    a_blk = jnp.exp(mi - m_new)
                    m_s[h, cur, :] = m_new
                    l_s[h, cur, :] = a_old * l_s[h, cur, :] + a_blk * li
                    acc_s[h, cur, :] = a_old * acc_s[h, cur, :] + a_blk * oi

        if nb == 1:
            for r0 in range(0, dil, jobs_per_call):
                process([(0, r) for r in range(r0, min(r0 + jobs_per_call, dil))])
        elif dil >= jobs_per_call:
            for r0 in range(0, dil, jobs_per_call):
                def body(n, carry, r0=r0):
                    process([(n, r) for r in range(r0, r0 + jobs_per_call)])
                    return carry
                lax.fori_loop(0, nb, body, 0)
        else:
            for r in range(dil):
                def body(n2, carry, r=r):
                    process([(n2 * jobs_per_call + j, r) for j in range(jobs_per_call)])
                    return carry
                lax.fori_loop(0, nb // jobs_per_call, body, 0)

    for h in range(heads_per_block):
        o_ref[0, :, h * HEAD_DIM:(h + 1) * HEAD_DIM] = (acc_s[h] / l_s[h]).astype(o_ref.dtype)


def dilated_attention(q, k, v, n_groups, n_heads, heads_per_block=2, jobs_per_call=4):
    bsz, s, qw = q.shape
    gw = n_heads * HEAD_DIM
    w = heads_per_block * HEAD_DIM
    n_hb = n_heads // heads_per_block
    for window, dil in DILATION_PAIRS[:n_groups]:
        assert window // dil <= ATTN_BLOCK and s % (dil * ATTN_BLOCK) == 0
    slopes = jnp.exp2(-8.0 * jnp.arange(1, n_heads + 1, dtype=F32) / n_heads)
    slope_row = jnp.repeat(slopes, HEAD_DIM).reshape(1, gw)
    specs = [pl.BlockSpec((1, s, w), lambda b, hb, g=g: (b, 0, g * n_hb + hb)) for g in range(n_groups)]
    scratch = pltpu.VMEM((heads_per_block, s, HEAD_DIM), F32)
    return pl.pallas_call(
        functools.partial(_attn_kernel, n_groups=n_groups, heads_per_block=heads_per_block, seq=s,
                          jobs_per_call=jobs_per_call),
        out_shape=jax.ShapeDtypeStruct((bsz, s, gw), BF16),
        grid=(bsz, n_hb),
        in_specs=specs * 3 + [pl.BlockSpec((1, w), lambda b, hb: (0, hb))],
        out_specs=pl.BlockSpec((1, s, w), lambda b, hb: (b, 0, hb)),
        scratch_shapes=[scratch] * 6,
        compiler_params=_params("parallel", "parallel"),
        name="dilated_attn",
    )(*([q] * n_groups + [k] * n_groups + [v] * n_groups), slope_row)


PROJ_TM = 2048
PROJ_TN = 256
RESID_TM = 1024
RESID_TN = 512


def _row(vec):
    return vec.reshape(1, -1)


def _tile(pref, dim):
    return min(pref, dim)


def rwkv_layer(x, a_mix, sh_mix, gate_mix, mu, w_r, w_k, w_v, w_o, idx, w0, w1, w2, a0, a1, a2, g1, g2,
               k_k, k_a, r_k, ln_w, ln_b, v_first, v_mix):
    bsz, seq, d = x.shape
    m = bsz * seq
    tm, tn = _tile(PROJ_TM, seq), _tile(PROJ_TN, d)
    bf = lambda w: w.astype(BF16)
    xr, xw, xk, xv, xa, xg = (t.reshape(m, d) for t in norm_token_mix(x, a_mix, sh_mix, mu))
    r, = matmul(xr, w_r, _ep_plain, [F32], n=d, tm=tm, tn=tn, layer=idx)
    k, = matmul(xk, w_k, _ep_plain, [F32], n=d, tm=tm, tn=tn, layer=idx)
    v, = matmul(xv, w_v, _ep_plain, [F32], n=d, tm=tm, tn=tn, layer=idx)
    lw = lora(xw, bf(w1), bf(w2), _row(w0), act="tanh", post="log_decay")
    a = lora(xa, bf(a1), bf(a2), _row(a0), act="none", post="sigmoid")
    g = lora(xg, bf(g1), bf(g2), jnp.zeros((1, d), F32), act="sigmoid", post="none")
    if v_mix is None:
        v_first = v
        vf = v_gate = None
    else:
        v0, v1, v2 = v_mix
        v_gate = lora(xv, bf(v1), bf(v2), _row(v0), act="none", post="sigmoid")
        vf = v_first
    to3 = lambda t: None if t is None else t.reshape(bsz, seq, d)
    y = wkv7(to3(r), to3(k), to3(v), to3(lw), to3(a), to3(g), to3(vf), to3(v_gate),
             _row(k_k), _row(k_a), _row(r_k), _row(ln_w), _row(ln_b), width=_tile(2048, d))
    x2, = matmul(y.reshape(m, d), w_o, _ep_residual_gate, [F32], n=d, tm=_tile(RESID_TM, seq),
                 tn=_tile(RESID_TN, d), layer=idx,
                 extras=[(x.reshape(m, d), "tile"), (gate_mix, "batch")])
    return x2.reshape(bsz, seq, d), v_first


def attn_layer(x, a_mix, sh_mix, gate_mix, w_q, w_o, idx, q_gain, kv_params, k_shared, v_shared):
    bsz, seq, d = x.shape
    m = bsz * seq
    q_width = w_q.shape[2]
    n_groups = q_gain.shape[0]
    group_w = q_width // n_groups
    n_heads = group_w // HEAD_DIM
    tm, tn = _tile(PROJ_TM, seq), _tile(PROJ_TN, group_w)
    bf = lambda w: w.astype(BF16)
    if kv_params is not None:
        kv_norm, w_kv, k_gain = kv_params
        a_all = jnp.concatenate([a_mix, jnp.broadcast_to(kv_norm[None, None, :], (bsz, 1, d))], axis=1)
        b_all = jnp.concatenate([sh_mix, jnp.zeros((bsz, 1, d), F32)], axis=1)
        h, hkv = norm_affine(x, a_all, b_all)
        k_gain_row = jnp.tile(k_gain, (1, n_heads)).reshape(1, q_width)
        k_shared, = matmul(hkv.reshape(m, d), w_kv, _ep_k_norm, [BF16], n=q_width, tm=tm, tn=tn,
                           extras=[(k_gain_row, "col")])
        v_shared, = matmul(hkv.reshape(m, d), w_kv, _ep_plain, [BF16], n=q_width, col_off=q_width,
                           tm=tm, tn=tn)
        k_shared = k_shared.reshape(bsz, seq, q_width)
        v_shared = v_shared.reshape(bsz, seq, q_width)
    else:
        h, = norm_affine(x, a_mix, sh_mix)
    q_gain_row = jnp.tile(q_gain, (1, n_heads)).reshape(1, q_width)
    q, = matmul(h.reshape(m, d), w_q, _ep_q_norm, [BF16], n=q_width, tm=tm, tn=tn, layer=idx,
                extras=[(q_gain_row, "col")])
    q = q.reshape(bsz, seq, q_width)
    merged = dilated_attention(q, k_shared, v_shared, n_groups, n_heads).reshape(m, group_w)
    x2, = matmul(merged, w_o, _ep_residual_gate, [F32], n=d, tm=_tile(RESID_TM, seq),
                 tn=_tile(RESID_TN, d), layer=idx,
                 extras=[(x.reshape(m, d), "tile"), (gate_mix, "batch")])
    return x2.reshape(bsz, seq, d), k_shared, v_shared


def ffn_layer(x, a_ffn, sh_ffn, gate_ffn, w_in_all, conv_w_all, layer, conv_b, w_out_all, ffn_tn=256, nk=1):
    bsz, seq, d = x.shape
    m = bsz * seq
    h, = norm_affine(x, a_ffn, sh_ffn)
    act = ffn_in(h.reshape(m, d), w_in_all, conv_w_all, _row(conv_b), layer, seq=seq,
                 tm=_tile(1024, seq), tn=ffn_tn)
    x2, = matmul(act, w_out_all, _ep_residual_gate, [F32], n=d, tm=_tile(512, seq), tn=_tile(512, d),
                 nk=nk, layer=layer, extras=[(x.reshape(m, d), "tile"), (gate_ffn, "batch")])
    return x2.reshape(bsz, seq, d)


def kernel(x, c, ada_w, ada_table, norm_mix, norm_ffn, rwkv_mu, rwkv_w_r, rwkv_w_k, rwkv_w_v, rwkv_w_o, rwkv_w0, rwkv_w1, rwkv_w2, rwkv_a0, rwkv_a1, rwkv_a2, rwkv_g1, rwkv_g2, rwkv_k_k, rwkv_k_a, rwkv_r_k, rwkv_ln_w, rwkv_ln_b, rwkv_v0, rwkv_v1, rwkv_v2, kv_norm, attn_w_kv, attn_k_gain, attn_w_q, attn_q_gain, attn_w_o, ffn_w_in, ffn_conv_w, ffn_conv_b, ffn_w_out):
    bsz, seq, d = x.shape
    depth = ada_table.shape[0]
    n_a = rwkv_w_r.shape[0]
    cond = ada_cond(c, ada_w).reshape(bsz, 6, d)
    ffn_w_out_bf = ffn_w_out.astype(BF16)
    v_first = None
    k_shared = v_shared = None
    for layer in range(depth):
        mod = cond + ada_table[layer][None]
        sh_mix, sc_mix, gate_mix, sh_ffn, sc_ffn, gate_ffn = (mod[:, i:i + 1, :] for i in range(6))
        a_mix = norm_mix[layer][None, None, :] * (1.0 + sc_mix)
        if layer < n_a:
            i = layer
            v_mix = None if i == 0 else (rwkv_v0[i - 1], rwkv_v1[i - 1], rwkv_v2[i - 1])
            x, v_first = rwkv_layer(
                x, a_mix, sh_mix, gate_mix, rwkv_mu[i], rwkv_w_r, rwkv_w_k, rwkv_w_v, rwkv_w_o, i,
                rwkv_w0[i], rwkv_w1[i], rwkv_w2[i], rwkv_a0[i], rwkv_a1[i], rwkv_a2[i], rwkv_g1[i], rwkv_g2[i],
                rwkv_k_k[i], rwkv_k_a[i], rwkv_r_k[i], rwkv_ln_w[i], rwkv_ln_b[i], v_first, v_mix)
        else:
            j = layer - n_a
            kv_params = (kv_norm, attn_w_kv, attn_k_gain) if j == 0 else None
            x, k_shared, v_shared = attn_layer(x, a_mix, sh_mix, gate_mix, attn_w_q, attn_w_o, j,
                                               attn_q_gain[j], kv_params, k_shared, v_shared)
        a_ffn = norm_ffn[layer][None, None, :] * (1.0 + sc_ffn)
        x = ffn_layer(x, a_ffn, sh_ffn, gate_ffn, ffn_w_in, ffn_conv_w, layer, ffn_conv_b[layer],
                      ffn_w_out_bf)
    return x
```

```python
import functools
import math

import jax
import jax.numpy as jnp
from jax import lax
from jax.experimental import pallas as pl
from jax.experimental.pallas import tpu as pltpu

F32 = jnp.float32
BF16 = jnp.bfloat16

VMEM_LIMIT_BYTES = 56 * 1024 * 1024
LANES = 128
SUBLANES = 8

NORM_EPS = 1e-6
GN_EPS = 64e-5
RWKV_HEAD = 64
WKV_CHUNK = 64
HEAD_DIM = 128
ATTN_BLOCK = 128
DILATION_PAIRS = ((128, 1), (512, 4), (2048, 16))
CONV_WIDTH = 3
NEG_BIG = -1e30


def _params(*sem):
    return pltpu.CompilerParams(dimension_semantics=sem, vmem_limit_bytes=VMEM_LIMIT_BYTES)


def _dot(a, b):
    return jnp.dot(a.astype(BF16), b.astype(BF16), preferred_element_type=F32)


def _dot_nt(a, b):
    return lax.dot_general(a.astype(BF16), b.astype(BF16), (((1,), (1,)), ((), ())),
                           preferred_element_type=F32)


def _dot_tn(a, b):
    return lax.dot_general(a.astype(BF16), b.astype(BF16), (((0,), (0,)), ((), ())),
                           preferred_element_type=F32)


def _split_hi_lo(x):
    hi = x.astype(BF16)
    lo = (x - hi.astype(F32)).astype(BF16)
    return hi, lo


def _ada_kernel(c_ref, w_ref, o_ref):
    c = c_ref[...]
    o_ref[...] = _dot(c * jax.nn.sigmoid(c), w_ref[...])


def ada_cond(c, ada_w, tn=512):
    b, d = c.shape
    n = ada_w.shape[1]
    cp = jnp.pad(c, ((0, SUBLANES - b), (0, 0)))
    out = pl.pallas_call(
        _ada_kernel,
        out_shape=jax.ShapeDtypeStruct((SUBLANES, n), F32),
        grid=(n // tn,),
        in_specs=[pl.BlockSpec((SUBLANES, d), lambda j: (0, 0)),
                  pl.BlockSpec((d, tn), lambda j: (0, j))],
        out_specs=pl.BlockSpec((SUBLANES, tn), lambda j: (0, j)),
        compiler_params=_params("parallel"),
        name="ada_cond",
    )(cp, ada_w)
    return out[:b]


NORM_ROWS = 16
NORM_COLS = 512


def _inv_rms(load_block, d):
    ss = None
    for j in range(d // NORM_COLS):
        xb = load_block(slice(j * NORM_COLS, (j + 1) * NORM_COLS))
        sq = xb * xb
        ss = sq if ss is None else ss + sq
    return lax.rsqrt(jnp.sum(ss, axis=-1, keepdims=True) * (1.0 / d) + NORM_EPS)


def _norm_affine_kernel(x_ref, a_ref, b_ref, *o_refs):
    x = x_ref[0]
    xn = x * lax.rsqrt(jnp.mean(x * x, axis=-1, keepdims=True) + NORM_EPS)
    for k, o_ref in enumerate(o_refs):
        o_ref[0] = (xn * a_ref[0, k:k + 1, :] + b_ref[0, k:k + 1, :]).astype(o_ref.dtype)


def norm_affine(x, a, b, tm=256):
    bsz, s, d = x.shape
    n_out = a.shape[1]
    outs = pl.pallas_call(
        _norm_affine_kernel,
        out_shape=[jax.ShapeDtypeStruct((bsz, s, d), BF16)] * n_out,
        grid=(bsz, s // tm),
        in_specs=[pl.BlockSpec((1, tm, d), lambda bi, i: (bi, i, 0)),
                  pl.BlockSpec((1, n_out, d), lambda bi, i: (bi, 0, 0)),
                  pl.BlockSpec((1, n_out, d), lambda bi, i: (bi, 0, 0))],
        out_specs=[pl.BlockSpec((1, tm, d), lambda bi, i: (bi, i, 0))] * n_out,
        compiler_params=_params("parallel", "parallel"),
        name="norm_affine",
    )(x, a, b)
    return outs


def _norm_mix_kernel(x_ref, xp_ref, a_ref, b_ref, mu_ref, *o_refs):
    i = pl.program_id(1)
    tm, d = x_ref.shape[1], x_ref.shape[2]
    row = lax.broadcasted_iota(jnp.int32, (NORM_ROWS, NORM_COLS), 0)

    def chunk(c, carry):
        r0 = pl.multiple_of(c * NORM_ROWS, NORM_ROWS)
        rows = pl.ds(r0, NORM_ROWS)
        before = pl.ds(pl.multiple_of(jnp.maximum(r0 - SUBLANES, 0), SUBLANES), SUBLANES)

        def shifted(cols):
            prev8 = jnp.where(c > 0, x_ref[0, before, cols], xp_ref[0, :, cols])
            return jnp.where(row == 0, prev8[SUBLANES - 1:SUBLANES, :],
                             pltpu.roll(x_ref[0, rows, cols], 1, axis=0))

        inv = _inv_rms(lambda cols: x_ref[0, rows, cols], d)
        inv_s = _inv_rms(shifted, d)
        seq_first = jnp.logical_and(jnp.logical_and(c == 0, i == 0), row == 0)
        for j in range(d // NORM_COLS):
            cols = slice(j * NORM_COLS, (j + 1) * NORM_COLS)
            a = a_ref[0, :, cols]
            b = b_ref[0, :, cols]
            h = x_ref[0, rows, cols] * inv * a + b
            h_prev = jnp.where(seq_first, 0.0, shifted(cols) * inv_s * a + b)
            xx = h_prev - h
            for k, o_ref in enumerate(o_refs):
                o_ref[0, rows, cols] = (h + xx * mu_ref[k:k + 1, cols]).astype(o_ref.dtype)
        return carry

    lax.fori_loop(0, tm // NORM_ROWS, chunk, 0)


def norm_token_mix(x, a, b, mu, tm=256):
    bsz, s, d = x.shape
    n_out = mu.shape[0]
    rows_per_blk = tm // SUBLANES
    outs = pl.pallas_call(
        _norm_mix_kernel,
        out_shape=[jax.ShapeDtypeStruct((bsz, s, d), BF16)] * n_out,
        grid=(bsz, s // tm),
        in_specs=[pl.BlockSpec((1, tm, d), lambda bi, i: (bi, i, 0)),
                  pl.BlockSpec((1, SUBLANES, d),
                               lambda bi, i: (bi, jnp.maximum(i * rows_per_blk - 1, 0), 0)),
                  pl.BlockSpec((1, 1, d), lambda bi, i: (bi, 0, 0)),
                  pl.BlockSpec((1, 1, d), lambda bi, i: (bi, 0, 0)),
                  pl.BlockSpec((n_out, d), lambda bi, i: (0, 0))],
        out_specs=[pl.BlockSpec((1, tm, d), lambda bi, i: (bi, i, 0))] * n_out,
        compiler_params=_params("parallel", "parallel"),
        name="norm_mix",
    )(x, x, a, b, mu)
    return outs


def _mm_kernel(x_ref, w_ref, *rest, epilogue, n_extra, nk):
    extras = rest[:n_extra]
    if nk == 1:
        outs = rest[n_extra:]
        epilogue(_dot(x_ref[...], w_ref[...]), extras, outs)
        return
    outs = rest[n_extra:-1]
    acc_ref = rest[-1]
    kk = pl.program_id(2)
    part = _dot(x_ref[...], w_ref[...])

    @pl.when(kk == 0)
    def _():
        acc_ref[...] = part

    @pl.when(kk > 0)
    def _():
        acc_ref[...] += part

    @pl.when(kk == nk - 1)
    def _():
        epilogue(acc_ref[...], extras, outs)


def matmul(x, w, epilogue, out_dtypes, *, n, col_off=0, tm, tn, nk=1, extras=(), layer=None):
    m, k = x.shape
    tk = k // nk
    assert m % tm == 0 and n % tn == 0 and col_off % tn == 0 and k % nk == 0
    joff = col_off // tn
    if layer is None:
        w_spec = pl.BlockSpec((tk, tn), lambda i, j, kk: (kk, j + joff))
    else:
        w_spec = pl.BlockSpec((None, tk, tn), lambda i, j, kk: (layer, kk, j + joff))
    in_specs = [pl.BlockSpec((tm, tk), lambda i, j, kk: (i, kk)), w_spec]
    args = [x, w]
    for arr, kind in extras:
        if kind == "tile":
            in_specs.append(pl.BlockSpec((tm, tn), lambda i, j, kk: (i, j)))
        elif kind == "col":
            in_specs.append(pl.BlockSpec((1, tn), lambda i, j, kk: (0, j)))
        else:
            seq = m // arr.shape[0]
            in_specs.append(pl.BlockSpec((1, 1, tn), lambda i, j, kk, seq=seq: ((i * tm) // seq, 0, j)))
        args.append(arr)
    scratch = [] if nk == 1 else [pltpu.VMEM((tm, tn), F32)]
    outs = pl.pallas_call(
        functools.partial(_mm_kernel, epilogue=epilogue, n_extra=len(extras), nk=nk),
        out_shape=[jax.ShapeDtypeStruct((m, n), dt) for dt in out_dtypes],
        grid=(m // tm, n // tn, nk),
        in_specs=in_specs,
        out_specs=[pl.BlockSpec((tm, tn), lambda i, j, kk: (i, j))] * len(out_dtypes),
        scratch_shapes=scratch,
        compiler_params=_params("parallel", "parallel", "arbitrary"),
        name="matmul_" + epilogue.__name__.strip("_"),
    )(*args)
    return outs


def _ep_plain(acc, extras, outs):
    outs[0][...] = acc.astype(outs[0].dtype)


def _ep_residual_gate(acc, extras, outs):
    resid_ref, gate_ref = extras
    outs[0][...] = resid_ref[...] + gate_ref[0] * acc


def _ep_head_rms(acc, extras, outs, scale):
    gain_ref, = extras
    tn = acc.shape[1]
    for h in range(tn // HEAD_DIM):
        sl = slice(h * HEAD_DIM, (h + 1) * HEAD_DIM)
        blk = acc[:, sl]
        y = blk * lax.rsqrt(jnp.mean(blk * blk, axis=-1, keepdims=True) + NORM_EPS) * gain_ref[:, sl]
        if scale != 1.0:
            y = y * scale
        outs[0][:, sl] = y.astype(outs[0].dtype)


def _ep_q_norm(acc, extras, outs):
    _ep_head_rms(acc, extras, outs, HEAD_DIM ** -0.5)


def _ep_k_norm(acc, extras, outs):
    _ep_head_rms(acc, extras, outs, 1.0)


def _lora_kernel(x_ref, w1_ref, w2_ref, bias_ref, o_ref, *, act, post):
    t = _dot(x_ref[...], w1_ref[...])
    if act == "tanh":
        t = jnp.tanh(t)
    elif act == "sigmoid":
        t = jax.nn.sigmoid(t)
    z = bias_ref[...] + _dot(t, w2_ref[...])
    if post == "sigmoid":
        z = jax.nn.sigmoid(z)
    elif post == "log_decay":
        z = -math.exp(-0.5) * jax.nn.sigmoid(z)
    o_ref[...] = z.astype(o_ref.dtype)


def lora(x, w1, w2, bias, *, act, post, out_dtype=F32, tm=512):
    m, k = x.shape
    r = w1.shape[1]
    n = w2.shape[1]
    tm = min(tm, m)
    return pl.pallas_call(
        functools.partial(_lora_kernel, act=act, post=post),
        out_shape=jax.ShapeDtypeStruct((m, n), out_dtype),
        grid=(m // tm,),
        in_specs=[pl.BlockSpec((tm, k), lambda i: (i, 0)),
                  pl.BlockSpec((k, r), lambda i: (0, 0)),
                  pl.BlockSpec((r, n), lambda i: (0, 0)),
                  pl.BlockSpec((1, n), lambda i: (0, 0))],
        out_specs=pl.BlockSpec((tm, n), lambda i: (i, 0)),
        compiler_params=_params("parallel"),
        name="lora_" + act + "_" + post,
    )(x, w1, w2, bias)


def _wkv_kernel(*refs, n_pairs, has_vmix):
    if has_vmix:
        (r_ref, k_ref, v_ref, lw_ref, a_ref, g_ref, vf_ref, vg_ref,
         kk_ref, ka_ref, rk_ref, lnw_ref, lnb_ref, o_ref, state_ref) = refs
    else:
        (r_ref, k_ref, v_ref, lw_ref, a_ref, g_ref,
         kk_ref, ka_ref, rk_ref, lnw_ref, lnb_ref, o_ref, state_ref) = refs
    C = WKV_CHUNK
    P2 = 2 * C
    c_idx = pl.program_id(2)

    @pl.when(c_idx == 0)
    def _():
        state_ref[...] = jnp.zeros_like(state_ref)

    lane = lax.broadcasted_iota(jnp.int32, (C, LANES), 1)
    lo_half = lane < RWKV_HEAD
    ri = lax.broadcasted_iota(jnp.int32, (P2, P2), 0)
    ci = lax.broadcasted_iota(jnp.int32, (P2, P2), 1)
    same_head = (ri < C) == (ci < C)
    strict = jnp.logical_and(same_head, ci < ri)
    incl = jnp.logical_and(same_head, ci <= ri)
    block_ones = jnp.where(same_head, 1.0, 0.0).astype(BF16)
    tri_r = lax.broadcasted_iota(jnp.int32, (C, C), 0)
    tri_c = lax.broadcasted_iota(jnp.int32, (C, C), 1)
    tril_ones = jnp.where(tri_c <= tri_r, 1.0, 0.0).astype(BF16)
    ones_sq = jnp.ones((P2, P2), BF16)

    def stack(x):
        return jnp.concatenate([jnp.where(lo_half, x, 0.0), jnp.where(lo_half, 0.0, x)], axis=0)

    def head_sum(x):
        return jnp.dot(x.astype(BF16), block_ones, preferred_element_type=F32)

    pairs = range(n_pairs)
    cols = [slice(p * LANES, (p + 1) * LANES) for p in pairs]

    def each(fn, *lists):
        return [fn(*args) for args in zip(*lists)]

    r = [r_ref[0, :, cs] for cs in cols]
    k = [k_ref[0, :, cs] for cs in cols]
    v = [v_ref[0, :, cs] for cs in cols]
    lw = [lw_ref[0, :, cs] for cs in cols]
    a = [a_ref[0, :, cs] for cs in cols]
    if has_vmix:
        v = [vi + (vf_ref[0, :, cs] - vi) * vg_ref[0, :, cs] for vi, cs in zip(v, cols)]
    kkr = [ki * kk_ref[:, cs] for ki, cs in zip(k, cols)]
    nrm = each(lambda x: jnp.sqrt(head_sum(x * x)), kkr)
    kk = each(lambda x, n: x / jnp.maximum(n, 1e-12), kkr, nrm)
    k2 = [ki * (1.0 + (ai - 1.0) * ka_ref[:, cs]) for ki, ai, cs in zip(k, a, cols)]

    lw_split = each(_split_hi_lo, lw)
    cum = each(lambda hl: jnp.dot(tril_ones, hl[0], preferred_element_type=F32)
               + jnp.dot(tril_ones, hl[1], preferred_element_type=F32), lw_split)
    cum_end_col = each(lambda hl: lax.dot_general(jnp.concatenate(hl, axis=0), ones_sq,
                                                  (((0,), (0,)), ((), ())),
                                                  preferred_element_type=F32), lw_split)

    e_prev = each(lambda c, l: jnp.exp(c - l), cum, lw)
    e_cum = each(jnp.exp, cum)
    e_neg = each(lambda c: jnp.exp(-c), cum)
    e_tail = each(lambda c: jnp.exp(c[C - 1:C, :] - c), cum)
    b = each(lambda x, y: x * y, kk, a)
    a_t = each(lambda x, e: stack(-x * e), kk, e_prev)
    r_t = each(lambda x, e: stack(x * e), r, e_cum)
    b_t = each(lambda x, e: stack(x * e), b, e_neg)
    k_t = each(lambda x, e: stack(x * e), k2, e_neg)
    b_g = each(lambda x, e: stack(x * e), b, e_tail)
    k_g = each(lambda x, e: stack(x * e), k2, e_tail)
    v_s = each(stack, v)

    gram = each(lambda at, rt, bt, kt: _dot_nt(jnp.concatenate([at, rt], axis=0),
                                               jnp.concatenate([bt, kt], axis=0)), a_t, r_t, b_t, k_t)
    p_ab = each(lambda gm: jnp.where(strict, gm[:P2, :P2], 0.0), gram)
    p_ak = each(lambda gm: jnp.where(strict, gm[:P2, P2:], 0.0), gram)
    p_r = each(lambda gm: jnp.concatenate([jnp.where(incl, gm[P2:, :P2], 0.0),
                                           jnp.where(incl, gm[P2:, P2:], 0.0)], axis=1), gram)

    eye = jnp.where(ri == ci, 1.0, 0.0)
    qk = each(lambda pb: _dot(pb, pb), p_ab)
    tk = each(lambda pb: eye + pb, p_ab)
    n_fac = int(math.log2(C))
    for i in range(1, n_fac):
        if i + 1 < n_fac:
            both = each(lambda qi, ti: _dot(qi, jnp.concatenate([qi, ti], axis=1)), qk, tk)
            qk = each(lambda bo: bo[:, :P2], both)
            tk = each(lambda ti, bo: ti + bo[:, P2:], tk, both)
        else:
            tk = each(lambda ti, qi: ti + _dot(qi, ti), tk, qk)
    x = each(lambda ti, at, pk, vs: _dot(ti, jnp.concatenate([at, _dot(pk, vs)], axis=1)),
             tk, a_t, p_ak, v_s)

    rhs = each(lambda xi, vs: jnp.concatenate(
        [xi, jnp.concatenate([jnp.zeros_like(vs), vs], axis=1)], axis=0), x, v_s)
    ry = each(_dot, p_r, rhs)
    gh = each(lambda bg, kg, rh: _dot_tn(jnp.concatenate([bg, kg], axis=0), rh), b_g, k_g, rhs)

    m0 = [state_ref[p] for p in pairs]
    sm = each(lambda rt, ryi, ghi, m: _dot(jnp.concatenate([rt + ryi[:, :P2], ghi[:, :P2]], axis=0), m),
              r_t, ry, gh, m0)
    for p in pairs:
        state_ref[p] = jnp.exp(cum_end_col[p]) * m0[p] + sm[p][P2:] + gh[p][:, P2:]
    y_s = each(lambda s, ryi: s[:P2] + ryi[:, P2:], sm, ry)
    y = each(lambda ys: ys[:C] + ys[C:], y_s)

    mean = each(lambda yi: head_sum(yi) * (1.0 / RWKV_HEAD), y)
    d = each(lambda yi, mi: yi - mi, y, mean)
    var = each(lambda di: head_sum(di * di) * (1.0 / RWKV_HEAD), d)
    bonus_w = [head_sum(ri * ki * rk_ref[:, cs]) for ri, ki, cs in zip(r, k2, cols)]
    for p, cs in zip(pairs, cols):
        yn = d[p] * lax.rsqrt(var[p] + GN_EPS) * lnw_ref[:, cs] + lnb_ref[:, cs]
        o_ref[0, :, cs] = ((yn + bonus_w[p] * v[p]) * g_ref[0, :, cs]).astype(o_ref.dtype)


def wkv7(r, k, v, lw, a, g, v_first, v_gate, k_k, k_a, r_k, ln_w, ln_b, *, width=512):
    bsz, t, c = r.shape
    has_vmix = v_first is not None
    n_pairs = width // LANES
    seq_spec = pl.BlockSpec((1, WKV_CHUNK, width), lambda bi, hg, ci: (bi, ci, hg))
    ch_spec = pl.BlockSpec((1, width), lambda bi, hg, ci: (0, hg))
    seq_args = [r, k, v, lw, a, g] + ([v_first, v_gate] if has_vmix else [])
    ch_args = [k_k, k_a, r_k, ln_w, ln_b]
    return pl.pallas_call(
        functools.partial(_wkv_kernel, n_pairs=n_pairs, has_vmix=has_vmix),
        out_shape=jax.ShapeDtypeStruct((bsz, t, c), BF16),
        grid=(bsz, c // width, t // WKV_CHUNK),
        in_specs=[seq_spec] * len(seq_args) + [ch_spec] * len(ch_args),
        out_specs=seq_spec,
        scratch_shapes=[pltpu.VMEM((n_pairs, LANES, LANES), F32)],
        compiler_params=_params("parallel", "parallel", "arbitrary"),
        name="wkv7",
    )(*seq_args, *ch_args)


def _ffn_in_kernel(x_ref, wg_ref, wu_ref, cwg_ref, cwu_ref, cbg_ref, cbu_ref, o_ref,
                   w_bf, carry_g, carry_u, *, tiles_per_seq):
    i = pl.program_id(1)
    tn = wg_ref.shape[1]

    @pl.when(i == 0)
    def _():
        w_bf[:, :tn] = wg_ref[...].astype(BF16)
        w_bf[:, tn:] = wu_ref[...].astype(BF16)

    @pl.when(i % tiles_per_seq == 0)
    def _():
        carry_g[...] = jnp.zeros_like(carry_g)
        carry_u[...] = jnp.zeros_like(carry_u)

    x = x_ref[...]
    tm = x.shape[0]

    def conv(u, carry_ref, cw_ref, cb_ref):
        row = lax.broadcasted_iota(jnp.int32, u.shape, 0)
        last1 = carry_ref[SUBLANES - 1:SUBLANES, :]
        last2 = carry_ref[SUBLANES - 2:SUBLANES - 1, :]
        u1 = jnp.where(row == 0, last1, pltpu.roll(u, 1, axis=0))
        u2 = jnp.where(row == 0, last2, jnp.where(row == 1, last1, pltpu.roll(u, 2, axis=0)))
        carry_ref[...] = u[tm - SUBLANES:, :]
        return cw_ref[0:1, :] * u + cw_ref[1:2, :] * u1 + cw_ref[2:3, :] * u2 + cb_ref[...]

    u = jnp.dot(x, w_bf[...], preferred_element_type=F32)
    gate = conv(u[:, :tn], carry_g, cwg_ref, cbg_ref)
    up = conv(u[:, tn:], carry_u, cwu_ref, cbu_ref)
    o_ref[...] = (gate * jax.nn.sigmoid(gate) * up).astype(o_ref.dtype)


def ffn_in(h, w_in, conv_w, conv_b, layer, *, seq, tm=1024, tn=256):
    m, k = h.shape
    f = w_in.shape[2] // 2
    nf = f // tn
    return pl.pallas_call(
        functools.partial(_ffn_in_kernel, tiles_per_seq=seq // tm),
        out_shape=jax.ShapeDtypeStruct((m, f), BF16),
        grid=(nf, m // tm),
        in_specs=[pl.BlockSpec((tm, k), lambda j, i: (i, 0)),
                  pl.BlockSpec((None, k, tn), lambda j, i: (layer, 0, j)),
                  pl.BlockSpec((None, k, tn), lambda j, i: (layer, 0, j + nf)),
                  pl.BlockSpec((None, CONV_WIDTH, tn), lambda j, i: (layer, 0, j)),
                  pl.BlockSpec((None, CONV_WIDTH, tn), lambda j, i: (layer, 0, j + nf)),
                  pl.BlockSpec((1, tn), lambda j, i: (0, j)),
                  pl.BlockSpec((1, tn), lambda j, i: (0, j + nf))],
        out_specs=pl.BlockSpec((tm, tn), lambda j, i: (i, j)),
        scratch_shapes=[pltpu.VMEM((k, 2 * tn), BF16),
                        pltpu.VMEM((SUBLANES, tn), F32), pltpu.VMEM((SUBLANES, tn), F32)],
        compiler_params=_params("parallel", "arbitrary"),
        name="ffn_in",
    )(h, w_in, w_in, conv_w, conv_w, conv_b, conv_b)


def _attn_kernel(*refs, n_groups, heads_per_block, seq, jobs_per_call):
    q_refs = refs[0:n_groups]
    k_refs = refs[n_groups:2 * n_groups]
    v_refs = refs[2 * n_groups:3 * n_groups]
    slope_ref, o_ref, qf, kf, vf, m_s, l_s, acc_s = refs[3 * n_groups:]
    blk = ATTN_BLOCK
    qi = lax.broadcasted_iota(jnp.int32, (blk, 2 * blk), 0)
    kj = lax.broadcasted_iota(jnp.int32, (blk, 2 * blk), 1) - blk
    dist = qi - kj

    for g in range(n_groups):
        window, dil = DILATION_PAIRS[g]
        span = window // dil
        nb = seq // (dil * blk)
        in_window = jnp.logical_and(dist >= 0, dist <= span)
        if nb == 1:
            in_window = jnp.logical_and(in_window, kj >= 0)
        bias = [jnp.where(in_window,
                          slope_ref[0:1, h * HEAD_DIM:h * HEAD_DIM + 1] * (-dil * dist).astype(F32), NEG_BIG)
                for h in range(heads_per_block)]
        for h in range(heads_per_block):
            hs = slice(h * HEAD_DIM, (h + 1) * HEAD_DIM)
            qf[h] = q_refs[g][0, :, hs].astype(F32)
            kf[h] = k_refs[g][0, :, hs].astype(F32)
            vf[h] = v_refs[g][0, :, hs].astype(F32)

        def process(jobs, g=g, dil=dil, nb=nb, bias=bias):
            step = dil * blk
            chains = []
            for n, r in jobs:
                if isinstance(n, int):
                    base, base_prev = n * step, max(n - 1, 0) * step
                else:
                    base = pl.multiple_of(n * step, step)
                    base_prev = pl.multiple_of(jnp.maximum(n - 1, 0) * step, step)
                cur = pl.ds(base + r, blk, stride=dil)
                prev = pl.ds(base_prev + r, blk, stride=dil)
                no_prev = None if nb == 1 else jnp.logical_and(kj < 0, n == 0)
                chains += [(h, cur, prev, no_prev) for h in range(heads_per_block)]
            s = [_dot_nt(qf[h, cur, :], jnp.concatenate([kf[h, prev, :], kf[h, cur, :]], axis=0))
                 for h, cur, prev, _ in chains]
            s = [si + bias[h] for si, (h, _, _, _) in zip(s, chains)]
            if nb > 1:
                s = [jnp.where(no_prev, NEG_BIG, si) for si, (_, _, _, no_prev) in zip(s, chains)]
            m_blk = [jnp.max(si, axis=-1, keepdims=True) for si in s]
            p = [jnp.exp(si - mi) for si, mi in zip(s, m_blk)]
            l_blk = [jnp.broadcast_to(jnp.sum(pi, axis=-1, keepdims=True), (blk, HEAD_DIM)) for pi in p]
            o_blk = [_dot(pi, jnp.concatenate([vf[h, prev, :], vf[h, cur, :]], axis=0))
                     for pi, (h, cur, prev, _) in zip(p, chains)]
            m_blk = [jnp.broadcast_to(mi, (blk, HEAD_DIM)) for mi in m_blk]
            for (h, cur, _, _), mi, li, oi in zip(chains, m_blk, l_blk, o_blk):
                if g == 0:
                    m_s[h, cur, :] = mi
                    l_s[h, cur, :] = li
                    acc_s[h, cur, :] = oi
                else:
                    m_old = m_s[h, cur, :]
                    m_new = jnp.maximum(m_old, mi)
                    a_old = jnp.exp(m_old - m_new)
                    a_blk = jnp.exp(mi - m_new)
                    m_s[h, cur, :] = m_new
                    l_s[h, cur, :] = a_old * l_s[h, cur, :] + a_blk * li
                    acc_s[h, cur, :] = a_old * acc_s[h, cur, :] + a_blk * oi

        if nb == 1:
            for r0 in range(0, dil, jobs_per_call):
                process([(0, r) for r in range(r0, min(r0 + jobs_per_call, dil))])
        elif dil >= jobs_per_call:
            for r0 in range(0, dil, jobs_per_call):
                def body(n, carry, r0=r0):
                    process([(n, r) for r in range(r0, r0 + jobs_per_call)])
                    return carry
                lax.fori_loop(0, nb, body, 0)
        else:
            for r in range(dil):
                def body(n2, carry, r=r):
                    process([(n2 * jobs_per_call + j, r) for j in range(jobs_per_call)])
                    return carry
                lax.fori_loop(0, nb // jobs_per_call, body, 0)

    for h in range(heads_per_block):
        o_ref[0, :, h * HEAD_DIM:(h + 1) * HEAD_DIM] = (acc_s[h] / l_s[h]).astype(o_ref.dtype)


def dilated_attention(q, k, v, n_groups, n_heads, heads_per_block=2, jobs_per_call=4):
    bsz, s, qw = q.shape
    gw = n_heads * HEAD_DIM
    w = heads_per_block * HEAD_DIM
    n_hb = n_heads // heads_per_block
    for window, dil in DILATION_PAIRS[:n_groups]:
        assert window // dil <= ATTN_BLOCK and s % (dil * ATTN_BLOCK) == 0
    slopes = jnp.exp2(-8.0 * jnp.arange(1, n_heads + 1, dtype=F32) / n_heads)
    slope_row = jnp.repeat(slopes, HEAD_DIM).reshape(1, gw)
    specs = [pl.BlockSpec((1, s, w), lambda b, hb, g=g: (b, 0, g * n_hb + hb)) for g in range(n_groups)]
    scratch = pltpu.VMEM((heads_per_block, s, HEAD_DIM), F32)
    return pl.pallas_call(
        functools.partial(_attn_kernel, n_groups=n_groups, heads_per_block=heads_per_block, seq=s,
                          jobs_per_call=jobs_per_call),
        out_shape=jax.ShapeDtypeStruct((bsz, s, gw), BF16),
        grid=(bsz, n_hb),
        in_specs=specs * 3 + [pl.BlockSpec((1, w), lambda b, hb: (0, hb))],
        out_specs=pl.BlockSpec((1, s, w), lambda b, hb: (b, 0, hb)),
        scratch_shapes=[scratch] * 6,
        compiler_params=_params("parallel", "parallel"),
        name="dilated_attn",
    )(*([q] * n_groups + [k] * n_groups + [v] * n_groups), slope_row)


PROJ_TM = 2048
PROJ_TN = 256
RESID_TM = 1024
RESID_TN = 512


def _row(vec):
    return vec.reshape(1, -1)


def _tile(pref, dim):
    return min(pref, dim)


def rwkv_layer(x, a_mix, sh_mix, gate_mix, mu, w_r, w_k, w_v, w_o, idx, w0, w1, w2, a0, a1, a2, g1, g2,
               k_k, k_a, r_k, ln_w, ln_b, v_first, v_mix):
    bsz, seq, d = x.shape
    m = bsz * seq
    tm, tn = _tile(PROJ_TM, seq), _tile(PROJ_TN, d)
    bf = lambda w: w.astype(BF16)
    xr, xw, xk, xv, xa, xg = (t.reshape(m, d) for t in norm_token_mix(x, a_mix, sh_mix, mu))
    r, = matmul(xr, w_r, _ep_plain, [F32], n=d, tm=tm, tn=tn, layer=idx)
    k, = matmul(xk, w_k, _ep_plain, [F32], n=d, tm=tm, tn=tn, layer=idx)
    v, = matmul(xv, w_v, _ep_plain, [F32], n=d, tm=tm, tn=tn, layer=idx)
    lw = lora(xw, bf(w1), bf(w2), _row(w0), act="tanh", post="log_decay")
    a = lora(xa, bf(a1), bf(a2), _row(a0), act="none", post="sigmoid")
    g = lora(xg, bf(g1), bf(g2), jnp.zeros((1, d), F32), act="sigmoid", post="none")
    if v_mix is None:
        v_first = v
        vf = v_gate = None
    else:
        v0, v1, v2 = v_mix
        v_gate = lora(xv, bf(v1), bf(v2), _row(v0), act="none", post="sigmoid")
        vf = v_first
    to3 = lambda t: None if t is None else t.reshape(bsz, seq, d)
    y = wkv7(to3(r), to3(k), to3(v), to3(lw), to3(a), to3(g), to3(vf), to3(v_gate),
             _row(k_k), _row(k_a), _row(r_k), _row(ln_w), _row(ln_b), width=_tile(2048, d))
    x2, = matmul(y.reshape(m, d), w_o, _ep_residual_gate, [F32], n=d, tm=_tile(RESID_TM, seq),
                 tn=_tile(RESID_TN, d), layer=idx,
                 extras=[(x.reshape(m, d), "tile"), (gate_mix, "batch")])
    return x2.reshape(bsz, seq, d), v_first


def attn_layer(x, a_mix, sh_mix, gate_mix, w_q, w_o, idx, q_gain, kv_params, k_shared, v_shared):
    bsz, seq, d = x.shape
    m = bsz * seq
    q_width = w_q.shape[2]
    n_groups = q_gain.shape[0]
    group_w = q_width // n_groups
    n_heads = group_w // HEAD_DIM
    tm, tn = _tile(RESID_TM, seq), _tile(RESID_TN, group_w)
    bf = lambda w: w.astype(BF16)
    if kv_params is not None:
        kv_norm, w_kv, k_gain = kv_params
        a_all = jnp.concatenate([a_mix, jnp.broadcast_to(kv_norm[None, None, :], (bsz, 1, d))], axis=1)
        b_all = jnp.concatenate([sh_mix, jnp.zeros((bsz, 1, d), F32)], axis=1)
        h, hkv = norm_affine(x, a_all, b_all)
        k_gain_row = jnp.tile(k_gain, (1, n_heads)).reshape(1, q_width)
        k_shared, = matmul(hkv.reshape(m, d), w_kv, _ep_k_norm, [BF16], n=q_width, tm=tm, tn=tn,
                           extras=[(k_gain_row, "col")])
        v_shared, = matmul(hkv.reshape(m, d), w_kv, _ep_plain, [BF16], n=q_width, col_off=q_width,
                           tm=tm, tn=tn)
        k_shared = k_shared.reshape(bsz, seq, q_width)
        v_shared = v_shared.reshape(bsz, seq, q_width)
    else:
        h, = norm_affine(x, a_mix, sh_mix)
    q_gain_row = jnp.tile(q_gain, (1, n_heads)).reshape(1, q_width)
    q, = matmul(h.reshape(m, d), w_q, _ep_q_norm, [BF16], n=q_width, tm=tm, tn=tn, layer=idx,
                extras=[(q_gain_row, "col")])
    q = q.reshape(bsz, seq, q_width)
    merged = dilated_attention(q, k_shared, v_shared, n_groups, n_heads).reshape(m, group_w)
    x2, = matmul(merged, w_o, _ep_residual_gate, [F32], n=d, tm=_tile(RESID_TM, seq),
                 tn=_tile(RESID_TN, d), layer=idx,
                 extras=[(x.reshape(m, d), "tile"), (gate_mix, "batch")])
    return x2.reshape(bsz, seq, d), k_shared, v_shared


def ffn_layer(x, a_ffn, sh_ffn, gate_ffn, w_in_all, conv_w_all, layer, conv_b, w_out_all, ffn_tn=256, nk=1):
    bsz, seq, d = x.shape
    m = bsz * seq
    h, = norm_affine(x, a_ffn, sh_ffn)
    act = ffn_in(h.reshape(m, d), w_in_all, conv_w_all, _row(conv_b), layer, seq=seq,
                 tm=_tile(1024, seq), tn=ffn_tn)
    x2, = matmul(act, w_out_all, _ep_residual_gate, [F32], n=d, tm=_tile(512, seq), tn=_tile(512, d),
                 nk=nk, layer=layer, extras=[(x.reshape(m, d), "tile"), (gate_ffn, "batch")])
    return x2.reshape(bsz, seq, d)


def kernel(x, c, ada_w, ada_table, norm_mix, norm_ffn, rwkv_mu, rwkv_w_r, rwkv_w_k, rwkv_w_v, rwkv_w_o, rwkv_w0, rwkv_w1, rwkv_w2, rwkv_a0, rwkv_a1, rwkv_a2, rwkv_g1, rwkv_g2, rwkv_k_k, rwkv_k_a, rwkv_r_k, rwkv_ln_w, rwkv_ln_b, rwkv_v0, rwkv_v1, rwkv_v2, kv_norm, attn_w_kv, attn_k_gain, attn_w_q, attn_q_gain, attn_w_o, ffn_w_in, ffn_conv_w, ffn_conv_b, ffn_w_out):
    bsz, seq, d = x.shape
    depth = ada_table.shape[0]
    n_a = rwkv_w_r.shape[0]
    cond = ada_cond(c, ada_w).reshape(bsz, 6, d)
    ffn_w_out_bf = ffn_w_out.astype(BF16)
    v_first = None
    k_shared = v_shared = None
    for layer in range(depth):
        mod = cond + ada_table[layer][None]
        sh_mix, sc_mix, gate_mix, sh_ffn, sc_ffn, gate_ffn = (mod[:, i:i + 1, :] for i in range(6))
        a_mix = norm_mix[layer][None, None, :] * (1.0 + sc_mix)
        if layer < n_a:
            i = layer
            v_mix = None if i == 0 else (rwkv_v0[i - 1], rwkv_v1[i - 1], rwkv_v2[i - 1])
            x, v_first = rwkv_layer(
                x, a_mix, sh_mix, gate_mix, rwkv_mu[i], rwkv_w_r, rwkv_w_k, rwkv_w_v, rwkv_w_o, i,
                rwkv_w0[i], rwkv_w1[i], rwkv_w2[i], rwkv_a0[i], rwkv_a1[i], rwkv_a2[i], rwkv_g1[i], rwkv_g2[i],
                rwkv_k_k[i], rwkv_k_a[i], rwkv_r_k[i], rwkv_ln_w[i], rwkv_ln_b[i], v_first, v_mix)
        else:
            j = layer - n_a
            kv_params = (kv_norm, attn_w_kv, attn_k_gain) if j == 0 else None
            x, k_shared, v_shared = attn_layer(x, a_mix, sh_mix, gate_mix, attn_w_q, attn_w_o, j,
                                               attn_q_gain[j], kv_params, k_shared, v_shared)
        a_ffn = norm_ffn[layer][None, None, :] * (1.0 + sc_ffn)
        x = ffn_layer(x, a_ffn, sh_ffn, gate_ffn, ffn_w_in, ffn_conv_w, layer, ffn_conv_b[layer],
                      ffn_w_out_bf)
    return x
```
